```python
import jax
import jax.numpy as jnp
from jax import lax
import numpy as np

D_MODEL = 2048
BATCH = 16
SEQ = 2048
DEPTH = 2
DEC_BATCH = 2
DEC_SEQ = 16384
PAST_LEN = 128

N_BRANCH = 4
BRANCH_W = D_MODEL // 4
D_FF = 4 * D_MODEL
EPS = 1e-6
NEG_INF = -1e30

CONV_A_K = 31
GRID_W = 64
NA_HEADS = 8
NA_HEAD_DIM = BRANCH_W // NA_HEADS
NA_KH_MAX = 8
NA_KW = 16
POOL_WINDOWS = (2, 4, 8, 16)
POOL_GROUPS = len(POOL_WINDOWS)
POOL_GW = BRANCH_W // POOL_GROUPS
DN_HEADS = 4
DN_DK = BRANCH_W // DN_HEADS
DN_DV = BRANCH_W // DN_HEADS
DN_CONV_K = 5
DN_CHUNK = 64

OFF_A = 0
OFF_B = OFF_A + 2 * BRANCH_W
OFF_C = OFF_B + 3 * BRANCH_W
OFF_D = OFF_C + BRANCH_W
D_IN = OFF_D + 4 * BRANCH_W + 4 * DN_HEADS

kernel_name = 'hybrid_bidir_encoder_two_groups'


def rmsnorm(x, g):
    xf = x.astype(jnp.float32)
    y = xf * lax.rsqrt(jnp.mean(xf * xf, axis=-1, keepdims=True) + EPS)
    return (y * g.astype(jnp.float32)).astype(x.dtype)


def depthwise_conv(x, w):
    k = w.shape[0]
    return lax.conv_general_dilated(
        x, w[:, None, :].astype(x.dtype), window_strides=(1,), padding=[(k // 2, k // 2)],
        dimension_numbers=('NWC', 'WIO', 'NWC'), feature_group_count=x.shape[-1])


def conformer_conv(u, conv_w, conv_b, ln_g, ln_b):
    a = u[..., :BRANCH_W] * jax.nn.sigmoid(u[..., BRANCH_W:])
    a = (depthwise_conv(a, conv_w) + conv_b).astype(jnp.float32)
    mu = jnp.mean(a, axis=-1, keepdims=True)
    var = jnp.mean(jnp.square(a - mu), axis=-1, keepdims=True)
    a = (a - mu) * lax.rsqrt(var + EPS) * ln_g.astype(jnp.float32) + ln_b.astype(jnp.float32)
    return jax.nn.silu(a).astype(u.dtype)


def neighbourhood_attention(qkv, rpb):
    bsz, t_len, _ = qkv.shape
    rows = t_len // GRID_W
    kh = min(NA_KH_MAX, rows)
    q, k, v = (qkv[..., i * BRANCH_W:(i + 1) * BRANCH_W].reshape(bsz, rows, GRID_W, NA_HEADS, NA_HEAD_DIM)
               for i in range(3))
    r = jnp.arange(rows)
    row_idx = jnp.clip(r - kh // 2, 0, rows - kh)[:, None] + jnp.arange(kh)[None, :]
    k_band = k[:, row_idx]
    v_band = v[:, row_idx]
    scores = jnp.einsum('brqhd,brikhd->bhrqik', q, k_band).astype(jnp.float32) * (NA_HEAD_DIM ** -0.5)
    c = jnp.arange(GRID_W)
    col_start = jnp.clip(c - NA_KW // 2, 0, GRID_W - NA_KW)
    col_ok = (c[None, :] >= col_start[:, None]) & (c[None, :] < col_start[:, None] + NA_KW)
    dr = row_idx - r[:, None] + (NA_KH_MAX - 1)
    dc = jnp.clip(c[None, :] - c[:, None] + (NA_KW - 1), 0, 2 * NA_KW - 2)
    bias = rpb.astype(jnp.float32)[:, dr[:, None, :, None], dc[None, :, None, :]]
    scores = jnp.where(col_ok[:, None, :], scores + bias, NEG_INF)
    probs = jax.nn.softmax(scores, axis=(-2, -1)).astype(qkv.dtype)
    out = jnp.einsum('bhrqik,brikhd->brqhd', probs, v_band)
    return out.reshape(bsz, t_len, BRANCH_W)


def multiscale_pool(xc, pool_w, pool_scale):
    bsz, t_len, _ = xc.shape
    xf = xc.astype(jnp.float32)
    csum = jnp.concatenate([jnp.zeros((bsz, 1, BRANCH_W), jnp.float32), jnp.cumsum(xf, axis=1)], axis=1)
    t = jnp.arange(t_len)
    groups = []
    for gi, win in enumerate(POOL_WINDOWS):
        lo = jnp.clip(t - win // 2, 0, t_len)
        hi = jnp.clip(t + win // 2, 0, t_len)
        sl = slice(gi * POOL_GW, (gi + 1) * POOL_GW)
        mean = (csum[:, hi, sl] - csum[:, lo, sl]) / (hi - lo).astype(jnp.float32)[None, :, None]
        groups.append(mean - xf[:, :, sl])
    pooled = jnp.stack(groups, axis=2).astype(xc.dtype)
    mixed = jnp.einsum('btgc,gce->btge', pooled, pool_w).reshape(bsz, t_len, BRANCH_W)
    return mixed * pool_scale


def chunk_gated_delta_rule(q, k, v, g, beta):
    bsz, t_len, nh, dk = q.shape
    dv = v.shape[-1]
    n = t_len // DN_CHUNK

    def chunks(a):
        a = a.astype(jnp.float32).reshape((bsz, n, DN_CHUNK, nh) + a.shape[3:])
        return jnp.moveaxis(a, 3, 1)

    q, k, v, g, beta = (chunks(a) for a in (q, k, v, g, beta))
    gc = jnp.cumsum(g, axis=-1)
    tril = jnp.tril(jnp.ones((DN_CHUNK, DN_CHUNK), dtype=bool))
    strict = jnp.tril(jnp.ones((DN_CHUNK, DN_CHUNK), dtype=bool), k=-1)
    decay = jnp.exp(jnp.where(tril, gc[..., :, None] - gc[..., None, :], NEG_INF))
    k_beta = k * beta[..., None]
    v_beta = v * beta[..., None]
    a_low = jnp.where(strict, jnp.einsum('bhncd,bhnsd->bhncs', k_beta, k) * decay, 0.0)
    eye = jnp.eye(DN_CHUNK, dtype=jnp.float32)
    t_mat = lax.linalg.triangular_solve(eye + a_low, jnp.broadcast_to(eye, a_low.shape),
                                        left_side=True, lower=True, unit_diagonal=True)
    u_val = t_mat @ v_beta
    w_k = t_mat @ (k_beta * jnp.exp(gc)[..., None])
    attn = jnp.einsum('bhncd,bhnsd->bhncs', q, k) * decay
    q_dec = q * jnp.exp(gc)[..., None]
    k_tail = k * jnp.exp(gc[..., -1:] - gc)[..., None]
    g_tot = jnp.exp(gc[..., -1])

    def step(state, xs):
        attn_c, w_c, u_c, q_c, k_c, g_c = xs
        v_new = u_c - w_c @ state
        out = q_c @ state + attn_c @ v_new
        state = state * g_c[..., None, None] + jnp.swapaxes(k_c, -1, -2) @ v_new
        return state, out

    xs = tuple(jnp.moveaxis(a, 2, 0) for a in (attn, w_k, u_val, q_dec, k_tail, g_tot))
    state0 = jnp.zeros((bsz, nh, dk, dv), jnp.float32)
    _, out = lax.scan(step, state0, xs)
    return jnp.transpose(out, (1, 0, 3, 2, 4)).reshape(bsz, t_len, nh, dv)


def flip_seq(a):
    return jnp.flip(a, axis=1)


def gated_deltanet(u, conv_w, a_log, dt_bias, norm_g):
    bsz, t_len, _ = u.shape
    qkv = jax.nn.silu(depthwise_conv(u[..., :3 * BRANCH_W], conv_w))

    def heads(a, d):
        return a.reshape(bsz, t_len, DN_HEADS, d).astype(jnp.float32)

    q = heads(qkv[..., :BRANCH_W], DN_DK)
    k = heads(qkv[..., BRANCH_W:2 * BRANCH_W], DN_DK)
    v = heads(qkv[..., 2 * BRANCH_W:], DN_DV)
    q = q * lax.rsqrt(jnp.sum(q * q, axis=-1, keepdims=True) + EPS) * (DN_DK ** -0.5)
    k = k * lax.rsqrt(jnp.sum(k * k, axis=-1, keepdims=True) + EPS)
    gate = heads(u[..., 3 * BRANCH_W:4 * BRANCH_W], DN_DV)
    b_raw = u[..., 4 * BRANCH_W:4 * BRANCH_W + 2 * DN_HEADS].reshape(bsz, t_len, 2, DN_HEADS).astype(jnp.float32)
    a_raw = u[..., 4 * BRANCH_W + 2 * DN_HEADS:].reshape(bsz, t_len, 2, DN_HEADS).astype(jnp.float32)
    beta = jax.nn.sigmoid(b_raw)
    g = -jnp.exp(a_log.astype(jnp.float32)) * jax.nn.softplus(a_raw + dt_bias.astype(jnp.float32))
    o_fwd = chunk_gated_delta_rule(q, k, v, g[:, :, 0], beta[:, :, 0])
    o_bwd = flip_seq(chunk_gated_delta_rule(flip_seq(q), flip_seq(k), flip_seq(v),
                                            flip_seq(g[:, :, 1]), flip_seq(beta[:, :, 1])))
    o = o_fwd + o_bwd
    o = o * lax.rsqrt(jnp.mean(o * o, axis=-1, keepdims=True) + EPS) * norm_g.astype(jnp.float32) * jax.nn.silu(gate)
    return o.reshape(bsz, t_len, BRANCH_W).astype(u.dtype)


def hybrid_layer(x, c, norm_mix_g, norm_ffn_g, w_ada, b_ada, w_in, conv_a_w, conv_a_b, ln_a_g, ln_a_b,
                 nat_rpb, pool_w, pool_scale, dn_conv_w, dn_a_log, dn_dt_bias, dn_norm_g,
                 w_branch, w_gate, b_gate, w_out, w_ff1, w_ff2):
    mod = (c @ w_ada + b_ada).reshape(c.shape[0], 6, 1, D_MODEL)
    shift_m, scale_m, gate_m, shift_f, scale_f, gate_f = (mod[:, i] for i in range(6))
    h = rmsnorm(x, norm_mix_g) * (1 + scale_m) + shift_m
    u = h @ w_in
    branch_out = (
        conformer_conv(u[..., OFF_A:OFF_B], conv_a_w, conv_a_b, ln_a_g, ln_a_b),
        neighbourhood_attention(u[..., OFF_B:OFF_C], nat_rpb),
        multiscale_pool(u[..., OFF_C:OFF_D], pool_w, pool_scale),
        gated_deltanet(u[..., OFF_D:], dn_conv_w, dn_a_log, dn_dt_bias, dn_norm_g),
    )
    merged = 0
    for bi, y in enumerate(branch_out):
        gate = jax.nn.sigmoid(h @ w_gate[:, bi] + b_gate[bi])
        merged = merged + gate * (y @ w_branch[bi])
    x = x + gate_m * (merged @ w_out)
    h = rmsnorm(x, norm_ffn_g) * (1 + scale_f) + shift_f
    x = x + gate_f * (jnp.square(jax.nn.relu(h @ w_ff1)) @ w_ff2)
    return x


def encoder_trunk(x, c, norm_mix_g, norm_ffn_g, w_ada, b_ada, w_in, conv_a_w, conv_a_b, ln_a_g, ln_a_b,
                  nat_rpb, pool_w, pool_scale, dn_conv_w, dn_a_log, dn_dt_bias, dn_norm_g,
                  w_branch, w_gate, b_gate, w_out, w_ff1, w_ff2, final_norm_g):
    for l in range(DEPTH):
        x = hybrid_layer(x, c, norm_mix_g[l], norm_ffn_g[l], w_ada[l], b_ada[l], w_in[l], conv_a_w[l],
                         conv_a_b[l], ln_a_g[l], ln_a_b[l], nat_rpb[l], pool_w[l], pool_scale[l],
                         dn_conv_w[l], dn_a_log[l], dn_dt_bias[l], dn_norm_g[l], w_branch[l], w_gate[l],
                         b_gate[l], w_out[l], w_ff1[l], w_ff2[l])
    return rmsnorm(x, final_norm_g)


def setup_inputs(seed: int = 0) -> dict:
    key = jax.random.key(seed)
    ks = jax.random.split(key, 32)
    f32 = jnp.float32
    L = DEPTH
    D = D_MODEL

    def nrm(k, shape, scale):
        return jax.random.normal(k, shape, f32) * scale

    dt = jnp.exp(jax.random.uniform(ks[18], (L, 2, DN_HEADS), f32, float(np.log(1e-3)), float(np.log(1e-1))))
    return {
        'x_prompt': nrm(ks[0], (BATCH, SEQ, D), 1.0),
        'x_sample': nrm(ks[1], (DEC_BATCH, DEC_SEQ, D), 1.0),
        'c_prompt': nrm(ks[2], (BATCH, D), 1.0),
        'c_sample': nrm(ks[3], (DEC_BATCH, D), 1.0),
        'norm_mix_g': 1.0 + nrm(ks[4], (L, D), 0.02),
        'norm_ffn_g': 1.0 + nrm(ks[5], (L, D), 0.02),
        'w_ada': nrm(ks[6], (L, D, 6 * D), 0.3 * D ** -0.5),
        'b_ada': nrm(ks[7], (L, 6 * D), 0.02),
        'w_in': nrm(ks[8], (L, D, D_IN), D ** -0.5),
        'conv_a_w': nrm(ks[9], (L, CONV_A_K, BRANCH_W), CONV_A_K ** -0.5),
        'conv_a_b': nrm(ks[10], (L, BRANCH_W), 0.02),
        'ln_a_g': 1.0 + nrm(ks[11], (L, BRANCH_W), 0.02),
        'ln_a_b': nrm(ks[12], (L, BRANCH_W), 0.02),
        'nat_rpb': nrm(ks[13], (L, NA_HEADS, 2 * NA_KH_MAX - 1, 2 * NA_KW - 1), 0.1),
        'pool_w': nrm(ks[14], (L, POOL_GROUPS, POOL_GW, POOL_GW), POOL_GW ** -0.5),
        'pool_scale': 1.0 + nrm(ks[15], (L, BRANCH_W), 0.02),
        'dn_conv_w': nrm(ks[16], (L, DN_CONV_K, 3 * BRANCH_W), DN_CONV_K ** -0.5),
        'dn_a_log': jnp.log(jax.random.uniform(ks[17], (L, 2, DN_HEADS), f32, 1.0, 16.0)),
        'dn_dt_bias': dt + jnp.log(-jnp.expm1(-dt)),
        'dn_norm_g': 1.0 + nrm(ks[19], (L, DN_DV), 0.02),
        'w_branch': nrm(ks[20], (L, N_BRANCH, BRANCH_W, D), BRANCH_W ** -0.5),
        'w_gate': nrm(ks[21], (L, D, N_BRANCH, D), D ** -0.5),
        'b_gate': nrm(ks[22], (L, N_BRANCH, D), 0.02),
        'w_out': nrm(ks[23], (L, D, D), D ** -0.5),
        'w_ff1': nrm(ks[24], (L, D, D_FF), D ** -0.5),
        'w_ff2': nrm(ks[25], (L, D_FF, D), D_FF ** -0.5),
        'final_norm_g': 1.0 + nrm(ks[26], (D,), 0.02),
    }


def reference(x_prompt, x_sample, c_prompt, c_sample, norm_mix_g, norm_ffn_g, w_ada, b_ada, w_in,
              conv_a_w, conv_a_b, ln_a_g, ln_a_b, nat_rpb, pool_w, pool_scale, dn_conv_w, dn_a_log,
              dn_dt_bias, dn_norm_g, w_branch, w_gate, b_gate, w_out, w_ff1, w_ff2, final_norm_g):
    y_prompt = encoder_trunk(x_prompt, c_prompt, norm_mix_g, norm_ffn_g, w_ada, b_ada, w_in, conv_a_w,
                             conv_a_b, ln_a_g, ln_a_b, nat_rpb, pool_w, pool_scale, dn_conv_w, dn_a_log,
                             dn_dt_bias, dn_norm_g, w_branch, w_gate, b_gate, w_out, w_ff1, w_ff2, final_norm_g)
    y_sample = encoder_trunk(x_sample, c_sample, norm_mix_g, norm_ffn_g, w_ada, b_ada, w_in, conv_a_w,
                             conv_a_b, ln_a_g, ln_a_b, nat_rpb, pool_w, pool_scale, dn_conv_w, dn_a_log,
                             dn_dt_bias, dn_norm_g, w_branch, w_gate, b_gate, w_out, w_ff1, w_ff2, final_norm_g)
    return (y_prompt, y_sample)
```

```python
import functools

import jax
import jax.numpy as jnp
import numpy as np
from jax import lax
from jax.experimental import pallas as pl
from jax.experimental.pallas import tpu as pltpu

F32 = jnp.float32
BF16 = jnp.bfloat16

D_MODEL = 2048
N_BRANCH = 4
BRANCH_W = D_MODEL // 4
D_FF = 4 * D_MODEL
EPS = 1e-6
NEG_INF = -1e30

CONV_A_K = 31
GRID_W = 64
NA_HEADS = 8
NA_HEAD_DIM = BRANCH_W // NA_HEADS
NA_KH = 8
NA_KW = 16
POOL_WINDOWS = (2, 4, 8, 16)
POOL_GW = BRANCH_W // len(POOL_WINDOWS)
DN_HEADS = 4
DN_DK = BRANCH_W // DN_HEADS
DN_CONV_K = 5
DN_CHUNK = 64

U_MAIN = 10 * BRANCH_W
U_SMALL = 4 * DN_HEADS
LANES = 128
CB_A_VAL, CB_A_GATE, CB_B_Q, CB_B_K, CB_B_V, CB_C, CB_D_Q, CB_D_K, CB_D_V, CB_D_GATE = range(10)

VMEM_LIMIT = 56 * 1024 * 1024
ROW_TILE = 1024
NORM_ROWS = 32
HALO_TT = 256
HALO_A = 16
HALO_D = 8
NA_ROWS = 8


def _cparams(*sem):
    return pltpu.CompilerParams(dimension_semantics=sem, vmem_limit_bytes=VMEM_LIMIT)


def _sigmoid(x):
    return 1.0 / (1.0 + jnp.exp(-x))


def _silu(x):
    return x * _sigmoid(x)


def _dot(a, b):
    return jnp.dot(a, b, preferred_element_type=F32)


def _dot_nt(a, b):
    return lax.dot_general(a, b, (((1,), (1,)), ((), ())), preferred_element_type=F32)


def _ada_kernel(c_ref, w_ref, b_ref, o_ref):
    o_ref[...] = _dot(c_ref[...].astype(BF16), w_ref[...].astype(BF16)) + b_ref[...]


def _ada(c_all, w_ada, b_ada):
    nl, d, n6 = w_ada.shape
    nb = c_all.shape[0]
    tn = 512
    out = pl.pallas_call(
        _ada_kernel,
        grid=(nl, n6 // tn),
        in_specs=[
            pl.BlockSpec((nb, d), lambda l, j: (0, 0)),
            pl.BlockSpec((None, d, tn), lambda l, j: (l, 0, j)),
            pl.BlockSpec((None, 1, tn), lambda l, j: (l, 0, j)),
        ],
        out_specs=pl.BlockSpec((None, nb, tn), lambda l, j: (l, 0, j)),
        out_shape=jax.ShapeDtypeStruct((nl, nb, n6), F32),
        compiler_params=_cparams("arbitrary", "arbitrary"),
        name="ada_mod",
    )(c_all, w_ada, b_ada.reshape(nl, 1, n6))
    return out.reshape(nl, nb, 6, d)


def _norm_mod(x_ref, g_ref, mod_ref, shift_idx, scale_idx, h_ref):
    g = g_ref[...]
    scale1 = 1.0 + mod_ref[scale_idx:scale_idx + 1, :]
    shift = mod_ref[shift_idx:shift_idx + 1, :]

    def body(c, carry):
        r0 = pl.multiple_of(c * NORM_ROWS, NORM_ROWS)
        xf = x_ref[pl.ds(r0, NORM_ROWS), :]
        ms = jnp.mean(xf * xf, axis=-1, keepdims=True)
        y = xf * lax.rsqrt(ms + EPS) * g
        h_ref[pl.ds(r0, NORM_ROWS), :] = (y * scale1 + shift).astype(h_ref.dtype)
        return carry

    lax.fori_loop(0, x_ref.shape[0] // NORM_ROWS, body, 0)


def _inproj_kernel(x_ref, g_ref, mod_ref, w_ref, ws_ref, u_ref, us_ref, h_ref):
    @pl.when(pl.program_id(1) == 0)
    def _():
        _norm_mod(x_ref, g_ref, mod_ref, 0, 1, h_ref)
        us_ref[...] = _dot(h_ref[...], ws_ref[...])

    u_ref[...] = _dot(h_ref[...], w_ref[...])


def _inproj(x, g, mods, w_main, w_small, seq_len):
    n, d = x.shape
    tm, tn = ROW_TILE, 1024
    tps = seq_len // tm
    return pl.pallas_call(
        _inproj_kernel,
        grid=(n // tm, U_MAIN // tn),
        in_specs=[
            pl.BlockSpec((tm, d), lambda i, j: (i, 0)),
            pl.BlockSpec((1, d), lambda i, j: (0, 0)),
            pl.BlockSpec((None, 6, d), lambda i, j: (i // tps, 0, 0)),
            pl.BlockSpec((d, tn), lambda i, j: (0, j)),
            pl.BlockSpec((d, LANES), lambda i, j: (0, 0)),
        ],
        out_specs=[
            pl.BlockSpec((tm, tn), lambda i, j: (i, j)),
            pl.BlockSpec((tm, LANES), lambda i, j: (i, 0)),
            pl.BlockSpec((tm, d), lambda i, j: (i, 0)),
        ],
        out_shape=[jax.ShapeDtypeStruct((n, U_MAIN), F32), jax.ShapeDtypeStruct((n, LANES), F32),
                   jax.ShapeDtypeStruct((n, d), BF16)],
        compiler_params=_cparams("arbitrary", "arbitrary"),
        name="in_proj",
    )(x, g, mods, w_main, w_small)


def _halo_specs(col_block, tt, halo, nt, bsz):
    per = tt // halo
    last = bsz * nt * per - 1

    def main(b, t):
        return (b * nt + t, col_block)

    def prev(b, t):
        return (jnp.maximum((b * nt + t) * per - 1, 0), col_block)

    def nxt(b, t):
        return (jnp.minimum((b * nt + t + 1) * per, last), col_block)

    return [pl.BlockSpec((tt, BRANCH_W), main), pl.BlockSpec((halo, BRANCH_W), prev),
            pl.BlockSpec((halo, BRANCH_W), nxt)]


def _fill_ext(ext_ref, main, prev, nxt, halo, tt):
    t = pl.program_id(1)
    ext_ref[halo:halo + tt, :] = main
    ext_ref[0:halo, :] = jnp.where(t > 0, prev, 0.0)
    ext_ref[halo + tt:2 * halo + tt, :] = jnp.where(t < pl.num_programs(1) - 1, nxt, 0.0)


def _conva_kernel(v_ref, pv_ref, nv_ref, g_ref, pg_ref, ng_ref, w_ref, b_ref, lg_ref, lb_ref, o_ref, a_ref):
    tt = o_ref.shape[0]
    _fill_ext(a_ref, v_ref[...] * _sigmoid(g_ref[...]), pv_ref[...] * _sigmoid(pg_ref[...]),
              nv_ref[...] * _sigmoid(ng_ref[...]), HALO_A, tt)
    rows = 32
    off = HALO_A - CONV_A_K // 2
    for c in range(tt // rows):
        acc = jnp.zeros((rows, BRANCH_W), F32) + b_ref[...]
        for k in range(CONV_A_K):
            s = c * rows + k + off
            acc = acc + a_ref[s:s + rows, :] * w_ref[k:k + 1, :]
        mu = jnp.mean(acc, axis=-1, keepdims=True)
        dlt = acc - mu
        var = jnp.mean(dlt * dlt, axis=-1, keepdims=True)
        y = dlt * lax.rsqrt(var + EPS) * lg_ref[...] + lb_ref[...]
        o_ref[c * rows:(c + 1) * rows, :] = _silu(y).astype(o_ref.dtype)


def _mixer_a(u, conv_w, conv_b, ln_g, ln_b, bsz, seq_len):
    n = u.shape[0]
    tt = HALO_TT
    nt = seq_len // tt
    row = lambda b, t: (0, 0)
    return pl.pallas_call(
        _conva_kernel,
        grid=(bsz, nt),
        in_specs=_halo_specs(CB_A_VAL, tt, HALO_A, nt, bsz) + _halo_specs(CB_A_GATE, tt, HALO_A, nt, bsz) + [
            pl.BlockSpec((CONV_A_K, BRANCH_W), row),
            pl.BlockSpec((1, BRANCH_W), row), pl.BlockSpec((1, BRANCH_W), row), pl.BlockSpec((1, BRANCH_W), row),
        ],
        out_specs=pl.BlockSpec((tt, BRANCH_W), lambda b, t: (b * nt + t, 0)),
        out_shape=jax.ShapeDtypeStruct((n, BRANCH_W), BF16),
        scratch_shapes=[pltpu.VMEM((tt + 2 * HALO_A, BRANCH_W), F32)],
        compiler_params=_cparams("arbitrary", "arbitrary"),
        name="mixer_a_conv",
    )(u, u, u, u, u, u, conv_w, conv_b, ln_g, ln_b)


def _pool_kernel(x_ref, px_ref, nx_ref, pw_ref, ps_ref, o_ref, a_ref, *, seq_len):
    tt = o_ref.shape[0]
    _fill_ext(a_ref, x_ref[...], px_ref[...], nx_ref[...], HALO_A, tt)
    pos = pl.program_id(1) * tt + lax.broadcasted_iota(jnp.int32, (tt, 1), 0)
    for gi, win in enumerate(POOL_WINDOWS):
        half = win // 2
        lo, hi = gi * POOL_GW, (gi + 1) * POOL_GW
        acc = jnp.zeros((tt, POOL_GW), F32)
        for j in range(-half, half):
            acc = acc + a_ref[HALO_A + j:HALO_A + j + tt, lo:hi]
        cnt = (jnp.minimum(pos + half, seq_len) - jnp.maximum(pos - half, 0)).astype(F32)
        pooled = acc / cnt - x_ref[:, lo:hi]
        mixed = _dot(pooled.astype(BF16), pw_ref[gi])
        o_ref[:, lo:hi] = (mixed * ps_ref[:, lo:hi]).astype(o_ref.dtype)


def _mixer_c(u, pool_w, pool_scale, bsz, seq_len):
    n = u.shape[0]
    tt = HALO_TT
    nt = seq_len // tt
    return pl.pallas_call(
        functools.partial(_pool_kernel, seq_len=seq_len),
        grid=(bsz, nt),
        in_specs=_halo_specs(CB_C, tt, HALO_A, nt, bsz) + [
            pl.BlockSpec((len(POOL_WINDOWS), POOL_GW, POOL_GW), lambda b, t: (0, 0, 0)),
            pl.BlockSpec((1, BRANCH_W), lambda b, t: (0, 0)),
        ],
        out_specs=pl.BlockSpec((tt, BRANCH_W), lambda b, t: (b * nt + t, 0)),
        out_shape=jax.ShapeDtypeStruct((n, BRANCH_W), BF16),
        scratch_shapes=[pltpu.VMEM((tt + 2 * HALO_A, BRANCH_W), F32)],
        compiler_params=_cparams("arbitrary", "arbitrary"),
        name="mixer_c_pool",
    )(u, u, u, pool_w, pool_scale)


def _na_bias_table(rpb):
    c = np.arange(GRID_W)
    col_start = np.clip(c - NA_KW // 2, 0, GRID_W - NA_KW)
    col_ok = (c[None, :] >= col_start[:, None]) & (c[None, :] < col_start[:, None] + NA_KW)
    dc = np.clip(c[None, :] - c[:, None] + (NA_KW - 1), 0, 2 * NA_KW - 2)
    case = np.arange(NA_KH)[:, None] + np.arange(NA_KH)[None, :]
    tab = rpb.astype(F32)[:, case][:, :, :, dc]
    tab = jnp.where(col_ok[None, None, None], tab, NEG_INF)
    tab = jnp.transpose(tab, (1, 0, 3, 2, 4))
    return tab.reshape(NA_KH, NA_HEADS, GRID_W, NA_KH * GRID_W)


def _na_kernel(q_ref, kp_ref, kc_ref, kn_ref, vp_ref, vc_ref, vn_ref, bt_ref, o_ref, k_ext, v_ext, *, n_rows):
    blk = NA_ROWS * GRID_W
    win = NA_KH * GRID_W
    j = pl.program_id(1)
    for i, (kr, vr) in enumerate(((kp_ref, vp_ref), (kc_ref, vc_ref), (kn_ref, vn_ref))):
        k_ext[i * blk:(i + 1) * blk, :] = kr[...].astype(BF16)
        v_ext[i * blk:(i + 1) * blk, :] = vr[...].astype(BF16)
    lane = lax.broadcasted_iota(jnp.int32, (1, LANES), 1)
    first_head = lane < NA_HEAD_DIM

    def row_body(ri, carry):
        r = j * NA_ROWS + ri
        rs = jnp.clip(r - NA_KH // 2, 0, n_rows - NA_KH)
        case = rs - r + (NA_KH - 1)
        start = pl.multiple_of((rs - j * NA_ROWS + NA_ROWS) * GRID_W, GRID_W)
        q0 = pl.multiple_of(ri * GRID_W, GRID_W)
        for hp in range(NA_HEADS // 2):
            lo, hi = hp * LANES, (hp + 1) * LANES
            qp = q_ref[pl.ds(q0, GRID_W), lo:hi] * (NA_HEAD_DIM ** -0.5)
            kw = k_ext[pl.ds(start, win), lo:hi]
            vw = v_ext[pl.ds(start, win), lo:hi]
            outs = []
            for sub in range(2):
                keep = first_head if sub == 0 else jnp.logical_not(first_head)
                qh = jnp.where(keep, qp, 0.0).astype(BF16)
                s = _dot_nt(qh, kw) + bt_ref[case, 2 * hp + sub]
                m = jnp.max(s, axis=-1, keepdims=True)
                p = jnp.exp(s - m)
                l = jnp.sum(p, axis=-1, keepdims=True)
                outs.append(_dot(p.astype(BF16), vw) / l)
            o_ref[pl.ds(q0, GRID_W), lo:hi] = jnp.where(first_head, outs[0], outs[1]).astype(o_ref.dtype)
        return carry

    lax.fori_loop(0, NA_ROWS, row_body, 0)


def _mixer_b(u, bias_tab, bsz, seq_len):
    n = u.shape[0]
    blk = NA_ROWS * GRID_W
    n_rows = seq_len // GRID_W
    ng = seq_len // blk

    def cur(col):
        return lambda b, j: (b * ng + j, col)

    def prev(col):
        return lambda b, j: (b * ng + jnp.maximum(j - 1, 0), col)

    def nxt(col):
        return lambda b, j: (b * ng + jnp.minimum(j + 1, ng - 1), col)

    bs = lambda f: pl.BlockSpec((blk, BRANCH_W), f)
    return pl.pallas_call(
        functools.partial(_na_kernel, n_rows=n_rows),
        grid=(bsz, ng),
        in_specs=[bs(cur(CB_B_Q)), bs(prev(CB_B_K)), bs(cur(CB_B_K)), bs(nxt(CB_B_K)),
                  bs(prev(CB_B_V)), bs(cur(CB_B_V)), bs(nxt(CB_B_V)),
                  pl.BlockSpec(bias_tab.shape, lambda b, j: (0, 0, 0, 0))],
        out_specs=pl.BlockSpec((blk, BRANCH_W), lambda b, j: (b * ng + j, 0)),
        out_shape=jax.ShapeDtypeStruct((n, BRANCH_W), BF16),
        scratch_shapes=[pltpu.VMEM((3 * blk, BRANCH_W), BF16), pltpu.VMEM((3 * blk, BRANCH_W), BF16)],
        compiler_params=_cparams("arbitrary", "arbitrary"),
        name="mixer_b_na",
    )(u, u, u, u, u, u, u, bias_tab)


def _dprep_kernel(q_ref, pq_ref, nq_ref, k_ref, pk_ref, nk_ref, v_ref, pv_ref, nv_ref, us_ref,
                  w_ref, alog_ref, dt_ref, qo_ref, ko_ref, vo_ref, gb_ref, e_ref):
    tt = qo_ref.shape[0]
    off = HALO_D - DN_CONV_K // 2
    srcs = ((q_ref, pq_ref, nq_ref, qo_ref), (k_ref, pk_ref, nk_ref, ko_ref), (v_ref, pv_ref, nv_ref, vo_ref))
    for ci, (m_ref, p_ref, n_ref, out_ref) in enumerate(srcs):
        _fill_ext(e_ref, m_ref[...], p_ref[...], n_ref[...], HALO_D, tt)
        acc = jnp.zeros((tt, BRANCH_W), F32)
        for k in range(DN_CONV_K):
            acc = acc + e_ref[off + k:off + k + tt, :] * w_ref[k:k + 1, ci * BRANCH_W:(ci + 1) * BRANCH_W]
        y = _silu(acc)
        if ci == 2:
            out_ref[...] = y
        else:
            post = DN_DK ** -0.5 if ci == 0 else 1.0
            for h in range(DN_HEADS):
                yh = y[:, h * DN_DK:(h + 1) * DN_DK]
                inv = lax.rsqrt(jnp.sum(yh * yh, axis=-1, keepdims=True) + EPS)
                out_ref[:, h * DN_DK:(h + 1) * DN_DK] = yh * inv * post
    us = us_ref[...]
    lane = lax.broadcasted_iota(jnp.int32, us.shape, 1)
    z = us + dt_ref[...]
    softplus = jnp.maximum(z, 0.0) + jnp.log1p(jnp.exp(-jnp.abs(z)))
    gb_ref[...] = jnp.where(lane < 2 * DN_HEADS, _sigmoid(us),
                            jnp.where(lane < U_SMALL, -jnp.exp(alog_ref[...]) * softplus, 0.0))


def _mixer_d_prep(u, us, conv_w, alog_row, dt_row, bsz, seq_len):
    n = u.shape[0]
    tt = HALO_TT
    nt = seq_len // tt
    out_bs = pl.BlockSpec((tt, BRANCH_W), lambda b, t: (b * nt + t, 0))
    sm_bs = pl.BlockSpec((tt, LANES), lambda b, t: (b * nt + t, 0))
    row = lambda b, t: (0, 0)
    return pl.pallas_call(
        _dprep_kernel,
        grid=(bsz, nt),
        in_specs=(_halo_specs(CB_D_Q, tt, HALO_D, nt, bsz) + _halo_specs(CB_D_K, tt, HALO_D, nt, bsz)
                  + _halo_specs(CB_D_V, tt, HALO_D, nt, bsz)
                  + [sm_bs, pl.BlockSpec((DN_CONV_K, 3 * BRANCH_W), row),
                     pl.BlockSpec((1, LANES), row), pl.BlockSpec((1, LANES), row)]),
        out_specs=[out_bs, out_bs, out_bs, sm_bs],
        out_shape=[jax.ShapeDtypeStruct((n, BRANCH_W), F32)] * 3 + [jax.ShapeDtypeStruct((n, LANES), F32)],
        scratch_shapes=[pltpu.VMEM((tt + 2 * HALO_D, BRANCH_W), F32)],
        compiler_params=_cparams("arbitrary", "arbitrary"),
        name="mixer_d_prep",
    )(u, u, u, u, u, u, u, u, u, us, conv_w, alog_row, dt_row)


def _split(a):
    hi = a.astype(BF16)
    lo = (a - hi.astype(F32)).astype(BF16)
    return hi, lo


def _dot3(a, b):
    return _dot(a[0], b[0]) + _dot(a[0], b[1]) + _dot(a[1], b[0])


def _unit_tri_inverse(a_mat, eye):
    p = eye - a_mat
    pw = _split(a_mat)
    n = 2
    while n < DN_CHUNK:
        sq = _dot3(pw, pw)
        pw = _split(sq)
        p = p + _dot3(_split(p), pw)
        n *= 2
    return p


def _delta_chunk(q_ref, k_ref, v_ref, gb_ref, o_ref, s_ref, rev):
    c = DN_CHUNK
    ri = lax.broadcasted_iota(jnp.int32, (c, c), 0)
    ci = lax.broadcasted_iota(jnp.int32, (c, c), 1)
    incl = (ri <= ci) if rev else (ri >= ci)
    strict = (ri < ci) if rev else (ri > ci)
    eye = (ri == ci).astype(F32)
    gb = gb_ref[...]
    g1 = gb.astype(BF16)
    r1 = gb - g1.astype(F32)
    g2 = r1.astype(BF16)
    g3 = (r1 - g2.astype(F32)).astype(BF16)
    ones_tri = incl.astype(BF16)
    cs = _dot(ones_tri, g1) + _dot(ones_tri, g2) + _dot(ones_tri, g3)
    last = 0 if rev else c - 1
    cs_last = cs[last:last + 1, :]
    e_in = jnp.exp(cs)
    e_tail = jnp.exp(cs_last - cs)
    g_tot = jnp.exp(cs_last)
    cs_t = cs.T
    d0 = 1 if rev else 0
    for h in range(DN_HEADS):
        lb = d0 * DN_HEADS + h
        lg = 2 * DN_HEADS + lb
        sl = slice(h * DN_DK, (h + 1) * DN_DK)
        q = q_ref[:, sl]
        k = k_ref[:, sl]
        v = v_ref[:, sl]
        beta = gb[:, lb:lb + 1]
        decay = jnp.exp(jnp.where(incl, cs[:, lg:lg + 1] - cs_t[lg:lg + 1, :], NEG_INF))
        kb = k * beta
        k16 = k.astype(BF16)
        a_mat = jnp.where(strict, _dot_nt(kb.astype(BF16), k16) * decay, 0.0)
        t_mat = _unit_tri_inverse(a_mat, eye).astype(BF16)
        u_val = _dot(t_mat, (v * beta).astype(BF16))
        w_k = _dot(t_mat, (kb * e_in[:, lg:lg + 1]).astype(BF16))
        attn = _dot_nt(q.astype(BF16), k16) * decay
        q_dec = q * e_in[:, lg:lg + 1]
        k_tail = k * e_tail[:, lg:lg + 1]
        si = d0 * DN_HEADS + h
        state = s_ref[si]
        s16 = state.astype(BF16)
        v_new = u_val - _dot(w_k.astype(BF16), s16)
        vn16 = v_new.astype(BF16)
        o_ref[:, sl] = _dot(q_dec.astype(BF16), s16) + _dot(attn.astype(BF16), vn16)
        s_ref[si] = state * g_tot[:, lg:lg + 1] + _dot(k_tail.T.astype(BF16), vn16)


def _delta_kernel(qf, kf, vf, gf, qb, kb, vb, gbb, of_ref, ob_ref, s_ref):
    @pl.when(pl.program_id(1) == 0)
    def _():
        s_ref[...] = jnp.zeros(s_ref.shape, F32)

    _delta_chunk(qf, kf, vf, gf, of_ref, s_ref, False)
    _delta_chunk(qb, kb, vb, gbb, ob_ref, s_ref, True)


def _mixer_d_scan(q, k, v, gb, bsz, seq_len):
    n = q.shape[0]
    nc = seq_len // DN_CHUNK
    fwd = lambda b, m: (b * nc + m, 0)
    bwd = lambda b, m: (b * nc + nc - 1 - m, 0)
    wide = lambda f: pl.BlockSpec((DN_CHUNK, BRANCH_W), f)
    small = lambda f: pl.BlockSpec((DN_CHUNK, LANES), f)
    return pl.pallas_call(
        _delta_kernel,
        grid=(bsz, nc),
        in_specs=[wide(fwd), wide(fwd), wide(fwd), small(fwd), wide(bwd), wide(bwd), wide(bwd), small(bwd)],
        out_specs=[wide(fwd), wide(bwd)],
        out_shape=[jax.ShapeDtypeStruct((n, BRANCH_W), F32)] * 2,
        scratch_shapes=[pltpu.VMEM((2 * DN_HEADS, DN_DK, DN_DK), F32)],
        compiler_params=_cparams("arbitrary", "arbitrary"),
        name="mixer_d_scan",
    )(q, k, v, gb, q, k, v, gb)


def _dout_kernel(of_ref, ob_ref, gate_ref, ng_ref, o_ref):
    o = of_ref[...] + ob_ref[...]
    gate = _silu(gate_ref[...])
    for h in range(DN_HEADS):
        sl = slice(h * DN_DK, (h + 1) * DN_DK)
        oh = o[:, sl]
        inv = lax.rsqrt(jnp.mean(oh * oh, axis=-1, keepdims=True) + EPS)
        o_ref[:, sl] = (oh * inv * ng_ref[...] * gate[:, sl]).astype(o_ref.dtype)


def _mixer_d_out(o_f, o_b, u, norm_g):
    n = o_f.shape[0]
    tt = 512
    bs = pl.BlockSpec((tt, BRANCH_W), lambda i: (i, 0))
    return pl.pallas_call(
        _dout_kernel,
        grid=(n // tt,),
        in_specs=[bs, bs, pl.BlockSpec((tt, BRANCH_W), lambda i: (i, CB_D_GATE)),
                  pl.BlockSpec((1, DN_DK), lambda i: (0, 0))],
        out_specs=bs,
        out_shape=jax.ShapeDtypeStruct((n, BRANCH_W), BF16),
        compiler_params=_cparams("arbitrary"),
        name="mixer_d_out",
    )(o_f, o_b, u, norm_g)


def _merge_kernel(h_ref, ya, yb, yc, yd, wg0, wg1, wg2, wg3, wb_ref, bg_ref, o_ref):
    acc = None
    for b, (y_ref, wg_ref) in enumerate(zip((ya, yb, yc, yd), (wg0, wg1, wg2, wg3))):
        gate = _sigmoid(_dot(h_ref[...], wg_ref[...]) + bg_ref[b:b + 1, :])
        term = gate * _dot(y_ref[...], wb_ref[b])
        acc = term if acc is None else acc + term
    o_ref[...] = acc.astype(o_ref.dtype)


def _merge(h, ys, w_gate, w_branch, b_gate):
    n, d = h.shape
    tm, tn = ROW_TILE, 512
    nj = d // tn
    ybs = pl.BlockSpec((tm, BRANCH_W), lambda i, j: (i, 0))

    def wg_spec(b):
        return pl.BlockSpec((d, tn), lambda i, j: (0, b * nj + j))

    return pl.pallas_call(
        _merge_kernel,
        grid=(n // tm, nj),
        in_specs=[
            pl.BlockSpec((tm, d), lambda i, j: (i, 0)),
            ybs, ybs, ybs, ybs,
            wg_spec(0), wg_spec(1), wg_spec(2), wg_spec(3),
            pl.BlockSpec((N_BRANCH, BRANCH_W, tn), lambda i, j: (0, 0, j)),
            pl.BlockSpec((N_BRANCH, tn), lambda i, j: (0, j)),
        ],
        out_specs=pl.BlockSpec((tm, tn), lambda i, j: (i, j)),
        out_shape=jax.ShapeDtypeStruct((n, d), BF16),
        compiler_params=_cparams("arbitrary", "arbitrary"),
        name="branch_merge",
    )(h, *ys, w_gate, w_gate, w_gate, w_gate, w_branch, b_gate)


def _outproj_kernel(m_ref, w_ref, x_ref, mod_ref, o_ref):
    o_ref[...] = x_ref[...] + mod_ref[2:3, :] * _dot(m_ref[...], w_ref[...])


def _outproj(merged, w_out, x, mods, seq_len):
    n, d = x.shape
    tm, tn = ROW_TILE, 1024
    tps = seq_len // tm
    return pl.pallas_call(
        _outproj_kernel,
        grid=(n // tm, d // tn),
        in_specs=[
            pl.BlockSpec((tm, d), lambda i, j: (i, 0)),
            pl.BlockSpec((d, tn), lambda i, j: (0, j)),
            pl.BlockSpec((tm, tn), lambda i, j: (i, j)),
            pl.BlockSpec((None, 6, tn), lambda i, j: (i // tps, 0, j)),
        ],
        out_specs=pl.BlockSpec((tm, tn), lambda i, j: (i, j)),
        out_shape=jax.ShapeDtypeStruct((n, d), F32),
        compiler_params=_cparams("arbitrary", "arbitrary"),
        name="out_proj",
    )(merged, w_out, x, mods)


def _ffn_kernel(x_ref, g_ref, mod_ref, w1_ref, w2_ref, o_ref, h_ref):
    f = pl.program_id(1)

    @pl.when(f == 0)
    def _():
        _norm_mod(x_ref, g_ref, mod_ref, 3, 4, h_ref)

    a = jnp.maximum(_dot(h_ref[...], w1_ref[...]), 0.0)
    part = _dot((a * a).astype(BF16), w2_ref[...])

    @pl.when(f == 0)
    def _():
        o_ref[...] = part

    @pl.when(f > 0)
    def _():
        o_ref[...] += part

    @pl.when(f == pl.num_programs(1) - 1)
    def _():
        o_ref[...] = x_ref[...] + mod_ref[5:6, :] * o_ref[...]


def _ffn(x, g, mods, w1, w2, seq_len):
    n, d = x.shape
    tm, tf = ROW_TILE, 512
    tps = seq_len // tm
    return pl.pallas_call(
        _ffn_kernel,
        grid=(n // tm, D_FF // tf),
        in_specs=[
            pl.BlockSpec((tm, d), lambda i, f: (i, 0)),
            pl.BlockSpec((1, d), lambda i, f: (0, 0)),
            pl.BlockSpec((None, 6, d), lambda i, f: (i // tps, 0, 0)),
            pl.BlockSpec((d, tf), lambda i, f: (0, f)),
            pl.BlockSpec((tf, d), lambda i, f: (f, 0)),
        ],
        out_specs=pl.BlockSpec((tm, d), lambda i, f: (i, 0)),
        out_shape=jax.ShapeDtypeStruct((n, d), F32),
        scratch_shapes=[pltpu.VMEM((tm, d), BF16)],
        compiler_params=_cparams("arbitrary", "arbitrary"),
        name="ffn",
    )(x, g, mods, w1, w2)


def _final_norm_kernel(x_ref, g_ref, o_ref):
    xf = x_ref[...]
    ms = jnp.mean(xf * xf, axis=-1, keepdims=True)
    o_ref[...] = xf * lax.rsqrt(ms + EPS) * g_ref[...]


def _final_norm(x, g):
    n, d = x.shape
    tt = 256
    return pl.pallas_call(
        _final_norm_kernel,
        grid=(n // tt,),
        in_specs=[pl.BlockSpec((tt, d), lambda i: (i, 0)), pl.BlockSpec((1, d), lambda i: (0, 0))],
        out_specs=pl.BlockSpec((tt, d), lambda i: (i, 0)),
        out_shape=jax.ShapeDtypeStruct((n, d), F32),
        compiler_params=_cparams("arbitrary"),
        name="final_norm",
    )(x, g)


def _layer(x, mods, p, bsz, seq_len):
    u, us, h = _inproj(x, p["norm_mix_g"], mods, p["w_in_main"], p["w_in_small"], seq_len)
    y_a = _mixer_a(u, p["conv_a_w"], p["conv_a_b"], p["ln_a_g"], p["ln_a_b"], bsz, seq_len)
    y_b = _mixer_b(u, p["na_bias"], bsz, seq_len)
    y_c = _mixer_c(u, p["pool_w"], p["pool_scale"], bsz, seq_len)
    q, k, v, gb = _mixer_d_prep(u, us, p["dn_conv_w"], p["dn_alog_row"], p["dn_dt_row"], bsz, seq_len)
    o_f, o_b = _mixer_d_scan(q, k, v, gb, bsz, seq_len)
    y_d = _mixer_d_out(o_f, o_b, u, p["dn_norm_g"])
    merged = _merge(h, (y_a, y_b, y_c, y_d), p["w_gate"], p["w_branch"], p["b_gate"])
    x = _outproj(merged, p["w_out"], x, mods, seq_len)
    return _ffn(x, p["norm_ffn_g"], mods, p["w_ff1"], p["w_ff2"], seq_len)


def _layer_params(l, norm_mix_g, norm_ffn_g, w_in, conv_a_w, conv_a_b, ln_a_g, ln_a_b, nat_rpb, pool_w, pool_scale,
                  dn_conv_w, dn_a_log, dn_dt_bias, dn_norm_g, w_branch, w_gate, b_gate, w_out, w_ff1, w_ff2):
    d = D_MODEL
    row = lambda a: a.reshape(1, -1).astype(F32)
    small_pad = ((0, 0), (0, LANES - U_SMALL))
    lane_pad = lambda a: jnp.pad(a.reshape(1, -1).astype(F32), ((0, 0), (2 * DN_HEADS, LANES - U_SMALL)))
    return {
        "norm_mix_g": row(norm_mix_g[l]), "norm_ffn_g": row(norm_ffn_g[l]),
        "w_in_main": w_in[l][:, :U_MAIN].astype(BF16),
        "w_in_small": jnp.pad(w_in[l][:, U_MAIN:], small_pad).astype(BF16),
        "conv_a_w": conv_a_w[l].astype(F32), "conv_a_b": row(conv_a_b[l]),
        "ln_a_g": row(ln_a_g[l]), "ln_a_b": row(ln_a_b[l]),
        "na_bias": _na_bias_table(nat_rpb[l]),
        "pool_w": pool_w[l].astype(BF16), "pool_scale": row(pool_scale[l]),
        "dn_conv_w": dn_conv_w[l].astype(F32),
        "dn_alog_row": lane_pad(dn_a_log[l]), "dn_dt_row": lane_pad(dn_dt_bias[l]),
        "dn_norm_g": row(dn_norm_g[l]),
        "w_branch": w_branch[l].astype(BF16),
        "w_gate": w_gate[l].reshape(d, N_BRANCH * d).astype(BF16),
        "b_gate": b_gate[l].astype(F32),
        "w_out": w_out[l].astype(BF16), "w_ff1": w_ff1[l].astype(BF16), "w_ff2": w_ff2[l].astype(BF16),
    }


def kernel(x_prompt, x_sample, c_prompt, c_sample, norm_mix_g, norm_ffn_g, w_ada, b_ada, w_in, conv_a_w, conv_a_b,
           ln_a_g, ln_a_b, nat_rpb, pool_w, pool_scale, dn_conv_w, dn_a_log, dn_dt_bias, dn_norm_g, w_branch, w_gate,
           b_gate, w_out, w_ff1, w_ff2, final_norm_g):
    depth = w_in.shape[0]
    layers = [_layer_params(l, norm_mix_g, norm_ffn_g, w_in, conv_a_w, conv_a_b, ln_a_g, ln_a_b, nat_rpb, pool_w,
                            pool_scale, dn_conv_w, dn_a_log, dn_dt_bias, dn_norm_g, w_branch, w_gate, b_gate, w_out,
                            w_ff1, w_ff2) for l in range(depth)]
    n_prompt = c_prompt.shape[0]
    mods = _ada(jnp.concatenate([c_prompt, c_sample], axis=0), w_ada, b_ada)
    fg = final_norm_g.reshape(1, -1).astype(F32)
    outs = []
    for x, lo, hi in ((x_prompt, 0, n_prompt), (x_sample, n_prompt, mods.shape[1])):
        bsz, seq_len, d = x.shape
        xt = x.reshape(bsz * seq_len, d)
        for l in range(depth):
            xt = _layer(xt, mods[l, lo:hi], layers[l], bsz, seq_len)
        outs.append(_final_norm(xt, fg).reshape(bsz, seq_len, d))
    return tuple(outs)
```

```python
import functools

import jax
import jax.numpy as jnp
import numpy as np
from jax import lax
from jax.experimental import pallas as pl
from jax.experimental.pallas import tpu as pltpu

F32 = jnp.float32
BF16 = jnp.bfloat16

D_MODEL = 2048
N_BRANCH = 4
BRANCH_W = D_MODEL // 4
D_FF = 4 * D_MODEL
EPS = 1e-6
NEG_INF = -1e30

CONV_A_K = 31
GRID_W = 64
NA_HEADS = 8
NA_HEAD_DIM = BRANCH_W // NA_HEADS
NA_KH = 8
NA_KW = 16
POOL_WINDOWS = (2, 4, 8, 16)
POOL_GW = BRANCH_W // len(POOL_WINDOWS)
DN_HEADS = 4
DN_DK = BRANCH_W // DN_HEADS
DN_CONV_K = 5
DN_CHUNK = 64

U_MAIN = 10 * BRANCH_W
U_SMALL = 4 * DN_HEADS
LANES = 128
SUBLANES = 8
CB_A_VAL, CB_A_GATE, CB_B_Q, CB_B_K, CB_B_V, CB_C, CB_D_Q, CB_D_K, CB_D_V, CB_D_GATE = range(10)

VMEM_LIMIT = 56 * 1024 * 1024
ROW_TILE = 1024
NORM_ROWS = 32
HALO_TT = 256
HALO_A = 16
HALO_D = 8
NA_ROWS = 8


def _cparams(*sem):
    return pltpu.CompilerParams(dimension_semantics=sem, vmem_limit_bytes=VMEM_LIMIT)


def _sigmoid(x):
    return 1.0 / (1.0 + jnp.exp(-x))


def _silu(x):
    return x * _sigmoid(x)


def _dot(a, b):
    return jnp.dot(a, b, preferred_element_type=F32)


def _dot_nt(a, b):
    return lax.dot_general(a, b, (((1,), (1,)), ((), ())), preferred_element_type=F32)


def _ada_kernel(c_ref, w_ref, b_ref, o_ref):
    o_ref[...] = _dot(c_ref[...].astype(BF16), w_ref[...].astype(BF16)) + b_ref[...]


def _ada(c_all, w_ada, b_ada):
    nl, d, n6 = w_ada.shape
    nb = c_all.shape[0]
    tn = 512
    out = pl.pallas_call(
        _ada_kernel,
        grid=(nl, n6 // tn),
        in_specs=[
            pl.BlockSpec((nb, d), lambda l, j: (0, 0)),
            pl.BlockSpec((None, d, tn), lambda l, j: (l, 0, j)),
            pl.BlockSpec((None, 1, tn), lambda l, j: (l, 0, j)),
        ],
        out_specs=pl.BlockSpec((None, nb, tn), lambda l, j: (l, 0, j)),
        out_shape=jax.ShapeDtypeStruct((nl, nb, n6), F32),
        compiler_params=_cparams("arbitrary", "arbitrary"),
        name="ada_mod",
    )(c_all, w_ada, b_ada.reshape(nl, 1, n6))
    return out.reshape(nl, nb, 6, d)


def _norm_mod(x_ref, g_ref, mod_ref, shift_idx, scale_idx, h_ref):
    g = g_ref[...]
    scale1 = 1.0 + mod_ref[scale_idx:scale_idx + 1, :]
    shift = mod_ref[shift_idx:shift_idx + 1, :]

    def body(c, carry):
        r0 = pl.multiple_of(c * NORM_ROWS, NORM_ROWS)
        xf = x_ref[pl.ds(r0, NORM_ROWS), :]
        ms = jnp.mean(xf * xf, axis=-1, keepdims=True)
        y = xf * lax.rsqrt(ms + EPS) * g
        h_ref[pl.ds(r0, NORM_ROWS), :] = (y * scale1 + shift).astype(h_ref.dtype)
        return carry

    lax.fori_loop(0, x_ref.shape[0] // NORM_ROWS, body, 0)


def _inproj_kernel(x_ref, g_ref, mod_ref, w_ref, ws_ref, u_ref, us_ref, h_ref):
    @pl.when(pl.program_id(1) == 0)
    def _():
        _norm_mod(x_ref, g_ref, mod_ref, 0, 1, h_ref)
        us_ref[...] = _dot(h_ref[...], ws_ref[...])

    u_ref[...] = _dot(h_ref[...], w_ref[...])


def _inproj(x, g, mods, w_main, w_small, seq_len):
    n, d = x.shape
    tm, tn = ROW_TILE, 1024
    tps = seq_len // tm
    return pl.pallas_call(
        _inproj_kernel,
        grid=(n // tm, U_MAIN // tn),
        in_specs=[
            pl.BlockSpec((tm, d), lambda i, j: (i, 0)),
            pl.BlockSpec((1, d), lambda i, j: (0, 0)),
            pl.BlockSpec((None, 6, d), lambda i, j: (i // tps, 0, 0)),
            pl.BlockSpec((d, tn), lambda i, j: (0, j)),
            pl.BlockSpec((d, LANES), lambda i, j: (0, 0)),
        ],
        out_specs=[
            pl.BlockSpec((tm, tn), lambda i, j: (i, j)),
            pl.BlockSpec((tm, LANES), lambda i, j: (i, 0)),
            pl.BlockSpec((tm, d), lambda i, j: (i, 0)),
        ],
        out_shape=[jax.ShapeDtypeStruct((n, U_MAIN), F32), jax.ShapeDtypeStruct((n, LANES), F32),
                   jax.ShapeDtypeStruct((n, d), BF16)],
        compiler_params=_cparams("arbitrary", "arbitrary"),
        name="in_proj",
    )(x, g, mods, w_main, w_small)


def _halo_specs(col_block, tt, halo, nt, bsz):
    per = tt // halo
    last = bsz * nt * per - 1

    def main(b, t):
        return (b * nt + t, col_block)

    def prev(b, t):
        return (jnp.maximum((b * nt + t) * per - 1, 0), col_block)

    def nxt(b, t):
        return (jnp.minimum((b * nt + t + 1) * per, last), col_block)

    return [pl.BlockSpec((tt, BRANCH_W), main), pl.BlockSpec((halo, BRANCH_W), prev),
            pl.BlockSpec((halo, BRANCH_W), nxt)]


def _fill_ext(ext_ref, main, prev, nxt, halo, tt):
    t = pl.program_id(1)
    ext_ref[halo:halo + tt, :] = main
    ext_ref[0:halo, :] = jnp.where(t > 0, prev, 0.0)
    ext_ref[halo + tt:2 * halo + tt, :] = jnp.where(t < pl.num_programs(1) - 1, nxt, 0.0)


def _conva_kernel(v_ref, pv_ref, nv_ref, g_ref, pg_ref, ng_ref, w_ref, b_ref, lg_ref, lb_ref, o_ref, a_ref, sh_ref):
    tt = o_ref.shape[0]
    _fill_ext(a_ref, v_ref[...] * _sigmoid(g_ref[...]), pv_ref[...] * _sigmoid(pg_ref[...]),
              nv_ref[...] * _sigmoid(ng_ref[...]), HALO_A, tt)
    rows = 32
    off = HALO_A - CONV_A_K // 2
    span = sh_ref.shape[1]
    for ph in range(1, SUBLANES):
        sh_ref[ph - 1] = a_ref[ph:ph + span, :]
    for c in range(tt // rows):
        parts = [jnp.zeros((SUBLANES, BRANCH_W), F32) + b_ref[...] for _ in range(rows // SUBLANES)]
        for k in range(CONV_A_K):
            ph, base = (k + off) % SUBLANES, c * rows + (k + off) // SUBLANES * SUBLANES
            wk = w_ref[k]
            for i in range(len(parts)):
                r0 = base + i * SUBLANES
                tap = a_ref[r0:r0 + SUBLANES, :] if ph == 0 else sh_ref[ph - 1, r0:r0 + SUBLANES, :]
                parts[i] = parts[i] + tap * wk
        acc = jnp.concatenate(parts, axis=0)
        mu = jnp.mean(acc, axis=-1, keepdims=True)
        dlt = acc - mu
        var = jnp.mean(dlt * dlt, axis=-1, keepdims=True)
        y = dlt * lax.rsqrt(var + EPS) * lg_ref[...] + lb_ref[...]
        o_ref[c * rows:(c + 1) * rows, :] = _silu(y).astype(o_ref.dtype)


def _mixer_a(u, conv_w, conv_b, ln_g, ln_b, bsz, seq_len):
    n = u.shape[0]
    tt = HALO_TT
    nt = seq_len // tt
    row = lambda b, t: (0, 0)
    return pl.pallas_call(
        _conva_kernel,
        grid=(bsz, nt),
        in_specs=_halo_specs(CB_A_VAL, tt, HALO_A, nt, bsz) + _halo_specs(CB_A_GATE, tt, HALO_A, nt, bsz) + [
            pl.BlockSpec((CONV_A_K, SUBLANES, BRANCH_W), lambda b, t: (0, 0, 0)),
            pl.BlockSpec((1, BRANCH_W), row), pl.BlockSpec((1, BRANCH_W), row), pl.BlockSpec((1, BRANCH_W), row),
        ],
        out_specs=pl.BlockSpec((tt, BRANCH_W), lambda b, t: (b * nt + t, 0)),
        out_shape=jax.ShapeDtypeStruct((n, BRANCH_W), BF16),
        scratch_shapes=[pltpu.VMEM((tt + 2 * HALO_A, BRANCH_W), F32),
                        pltpu.VMEM((SUBLANES - 1, tt + 2 * HALO_A - SUBLANES, BRANCH_W), F32)],
        compiler_params=_cparams("arbitrary", "arbitrary"),
        name="mixer_a_conv",
    )(u, u, u, u, u, u, conv_w, conv_b, ln_g, ln_b)


def _pool_kernel(x_ref, px_ref, nx_ref, pw_ref, ps_ref, o_ref, a_ref, *, seq_len):
    tt = o_ref.shape[0]
    _fill_ext(a_ref, x_ref[...], px_ref[...], nx_ref[...], HALO_A, tt)
    pos = pl.program_id(1) * tt + lax.broadcasted_iota(jnp.int32, (tt, 1), 0)
    for gi, win in enumerate(POOL_WINDOWS):
        half = win // 2
        lo, hi = gi * POOL_GW, (gi + 1) * POOL_GW
        acc = jnp.zeros((tt, POOL_GW), F32)
        for j in range(-half, half):
            acc = acc + a_ref[HALO_A + j:HALO_A + j + tt, lo:hi]
        cnt = (jnp.minimum(pos + half, seq_len) - jnp.maximum(pos - half, 0)).astype(F32)
        pooled = acc / cnt - x_ref[:, lo:hi]
        mixed = _dot(pooled.astype(BF16), pw_ref[gi])
        o_ref[:, lo:hi] = (mixed * ps_ref[:, lo:hi]).astype(o_ref.dtype)


def _mixer_c(u, pool_w, pool_scale, bsz, seq_len):
    n = u.shape[0]
    tt = HALO_TT
    nt = seq_len // tt
    return pl.pallas_call(
        functools.partial(_pool_kernel, seq_len=seq_len),
        grid=(bsz, nt),
        in_specs=_halo_specs(CB_C, tt, HALO_A, nt, bsz) + [
            pl.BlockSpec((len(POOL_WINDOWS), POOL_GW, POOL_GW), lambda b, t: (0, 0, 0)),
            pl.BlockSpec((1, BRANCH_W), lambda b, t: (0, 0)),
        ],
        out_specs=pl.BlockSpec((tt, BRANCH_W), lambda b, t: (b * nt + t, 0)),
        out_shape=jax.ShapeDtypeStruct((n, BRANCH_W), BF16),
        scratch_shapes=[pltpu.VMEM((tt + 2 * HALO_A, BRANCH_W), F32)],
        compiler_params=_cparams("arbitrary", "arbitrary"),
        name="mixer_c_pool",
    )(u, u, u, pool_w, pool_scale)


def _na_bias_table(rpb):
    c = np.arange(GRID_W)
    col_start = np.clip(c - NA_KW // 2, 0, GRID_W - NA_KW)
    col_ok = (c[None, :] >= col_start[:, None]) & (c[None, :] < col_start[:, None] + NA_KW)
    dc = np.clip(c[None, :] - c[:, None] + (NA_KW - 1), 0, 2 * NA_KW - 2)
    case = np.arange(NA_KH)[:, None] + np.arange(NA_KH)[None, :]
    tab = rpb.astype(F32)[:, case][:, :, :, dc]
    tab = jnp.where(col_ok[None, None, None], tab, NEG_INF)
    tab = jnp.transpose(tab, (1, 0, 3, 2, 4))
    return tab.reshape(NA_KH, NA_HEADS // 2, 2 * GRID_W, NA_KH * GRID_W)


def _na_kernel(q_ref, kp_ref, kc_ref, kn_ref, vp_ref, vc_ref, vn_ref, bt_ref, o_ref, k_ext, v_ext, *, n_rows):
    blk = NA_ROWS * GRID_W
    win = NA_KH * GRID_W
    j = pl.program_id(1)
    for i, (kr, vr) in enumerate(((kp_ref, vp_ref), (kc_ref, vc_ref), (kn_ref, vn_ref))):
        k_ext[i * blk:(i + 1) * blk, :] = kr[...].astype(BF16)
        v_ext[i * blk:(i + 1) * blk, :] = vr[...].astype(BF16)
    lane = lax.broadcasted_iota(jnp.int32, (1, LANES), 1)
    first_head = lane < NA_HEAD_DIM

    def row_body(ri, carry):
        r = j * NA_ROWS + ri
        rs = jnp.clip(r - NA_KH // 2, 0, n_rows - NA_KH)
        case = rs - r + (NA_KH - 1)
        start = pl.multiple_of((rs - j * NA_ROWS + NA_ROWS) * GRID_W, GRID_W)
        q0 = pl.multiple_of(ri * GRID_W, GRID_W)
        pairs = range(NA_HEADS // 2)
        lanes = [slice(hp * LANES, (hp + 1) * LANES) for hp in pairs]
        scores = []
        for hp in pairs:
            qp = q_ref[pl.ds(q0, GRID_W), lanes[hp]] * (NA_HEAD_DIM ** -0.5)
            qs = jnp.concatenate([jnp.where(first_head, qp, 0.0), jnp.where(first_head, 0.0, qp)], axis=0)
            scores.append(_dot_nt(qs.astype(BF16), k_ext[pl.ds(start, win), lanes[hp]]) + bt_ref[case, hp])
        probs, norms = [], []
        for s in scores:
            p = jnp.exp(s - jnp.max(s, axis=-1, keepdims=True))
            probs.append(p.astype(BF16))
            norms.append(jnp.sum(p, axis=-1, keepdims=True))
        for hp in pairs:
            pv = _dot(probs[hp], v_ext[pl.ds(start, win), lanes[hp]]) / norms[hp]
            o_ref[pl.ds(q0, GRID_W), lanes[hp]] = jnp.where(first_head, pv[:GRID_W], pv[GRID_W:]).astype(o_ref.dtype)
        return carry

    lax.fori_loop(0, NA_ROWS, row_body, 0)


def _mixer_b(u, bias_tab, bsz, seq_len):
    n = u.shape[0]
    blk = NA_ROWS * GRID_W
    n_rows = seq_len // GRID_W
    ng = seq_len // blk

    def cur(col):
        return lambda b, j: (b * ng + j, col)

    def prev(col):
        return lambda b, j: (b * ng + jnp.maximum(j - 1, 0), col)

    def nxt(col):
        return lambda b, j: (b * ng + jnp.minimum(j + 1, ng - 1), col)

    bs = lambda f: pl.BlockSpec((blk, BRANCH_W), f)
    return pl.pallas_call(
        functools.partial(_na_kernel, n_rows=n_rows),
        grid=(bsz, ng),
        in_specs=[bs(cur(CB_B_Q)), bs(prev(CB_B_K)), bs(cur(CB_B_K)), bs(nxt(CB_B_K)),
                  bs(prev(CB_B_V)), bs(cur(CB_B_V)), bs(nxt(CB_B_V)),
                  pl.BlockSpec(bias_tab.shape, lambda b, j: (0, 0, 0, 0))],
        out_specs=pl.BlockSpec((blk, BRANCH_W), lambda b, j: (b * ng + j, 0)),
        out_shape=jax.ShapeDtypeStruct((n, BRANCH_W), BF16),
        scratch_shapes=[pltpu.VMEM((3 * blk, BRANCH_W), BF16), pltpu.VMEM((3 * blk, BRANCH_W), BF16)],
        compiler_params=_cparams("arbitrary", "arbitrary"),
        name="mixer_b_na",
    )(u, u, u, u, u, u, u, bias_tab)


def _dprep_kernel(q_ref, pq_ref, nq_ref, k_ref, pk_ref, nk_ref, v_ref, pv_ref, nv_ref, us_ref,
                  w_ref, alog_ref, dt_ref, qo_ref, ko_ref, vo_ref, gb_ref, e_ref):
    tt = qo_ref.shape[0]
    off = HALO_D - DN_CONV_K // 2
    srcs = ((q_ref, pq_ref, nq_ref, qo_ref), (k_ref, pk_ref, nk_ref, ko_ref), (v_ref, pv_ref, nv_ref, vo_ref))
    for ci, (m_ref, p_ref, n_ref, out_ref) in enumerate(srcs):
        _fill_ext(e_ref, m_ref[...], p_ref[...], n_ref[...], HALO_D, tt)
        acc = jnp.zeros((tt, BRANCH_W), F32)
        for k in range(DN_CONV_K):
            acc = acc + e_ref[off + k:off + k + tt, :] * w_ref[k:k + 1, ci * BRANCH_W:(ci + 1) * BRANCH_W]
        y = _silu(acc)
        if ci == 2:
            out_ref[...] = y
        else:
            post = DN_DK ** -0.5 if ci == 0 else 1.0
            for h in range(DN_HEADS):
                yh = y[:, h * DN_DK:(h + 1) * DN_DK]
                inv = lax.rsqrt(jnp.sum(yh * yh, axis=-1, keepdims=True) + EPS)
                out_ref[:, h * DN_DK:(h + 1) * DN_DK] = yh * inv * post
    us = us_ref[...]
    lane = lax.broadcasted_iota(jnp.int32, us.shape, 1)
    z = us + dt_ref[...]
    softplus = jnp.maximum(z, 0.0) + jnp.log1p(jnp.exp(-jnp.abs(z)))
    gb_ref[...] = jnp.where(lane < 2 * DN_HEADS, _sigmoid(us),
                            jnp.where(lane < U_SMALL, -jnp.exp(alog_ref[...]) * softplus, 0.0))


def _mixer_d_prep(u, us, conv_w, alog_row, dt_row, bsz, seq_len):
    n = u.shape[0]
    tt = HALO_TT
    nt = seq_len // tt
    out_bs = pl.BlockSpec((tt, BRANCH_W), lambda b, t: (b * nt + t, 0))
    sm_bs = pl.BlockSpec((tt, LANES), lambda b, t: (b * nt + t, 0))
    row = lambda b, t: (0, 0)
    return pl.pallas_call(
        _dprep_kernel,
        grid=(bsz, nt),
        in_specs=(_halo_specs(CB_D_Q, tt, HALO_D, nt, bsz) + _halo_specs(CB_D_K, tt, HALO_D, nt, bsz)
                  + _halo_specs(CB_D_V, tt, HALO_D, nt, bsz)
                  + [sm_bs, pl.BlockSpec((DN_CONV_K, 3 * BRANCH_W), row),
                     pl.BlockSpec((1, LANES), row), pl.BlockSpec((1, LANES), row)]),
        out_specs=[out_bs, out_bs, out_bs, sm_bs],
        out_shape=[jax.ShapeDtypeStruct((n, BRANCH_W), F32)] * 3 + [jax.ShapeDtypeStruct((n, LANES), F32)],
        scratch_shapes=[pltpu.VMEM((tt + 2 * HALO_D, BRANCH_W), F32)],
        compiler_params=_cparams("arbitrary", "arbitrary"),
        name="mixer_d_prep",
    )(u, u, u, u, u, u, u, u, u, us, conv_w, alog_row, dt_row)


def _dir_prep(gb_ref, rev):
    c = DN_CHUNK
    ri = lax.broadcasted_iota(jnp.int32, (c, LANES), 0)
    ci = lax.broadcasted_iota(jnp.int32, (c, LANES), 1) % c
    incl = (ri <= ci) if rev else (ri >= ci)
    strict = (ri < ci) if rev else (ri > ci)
    gb = gb_ref[...]
    g1 = gb.astype(BF16)
    r1 = gb - g1.astype(F32)
    g2 = r1.astype(BF16)
    g3 = (r1 - g2.astype(F32)).astype(BF16)
    tri = incl[:, :c].astype(BF16)
    cs = _dot(jnp.concatenate([tri, tri, tri], axis=1), jnp.concatenate([g1, g2, g3], axis=0))
    last = 0 if rev else c - 1
    cs_last = cs[last:last + 1, :]
    return dict(incl=incl, strict=strict, eye=(ri == ci).astype(F32), gb=gb, cs=cs,
                cs_t=jnp.concatenate([cs, cs], axis=0).T, e_in=jnp.exp(cs), e_tail=jnp.exp(cs_last - cs),
                g_tot=jnp.exp(cs_last))


def _delta_step(dirs, s_ref):
    c = DN_CHUNK
    left = lax.broadcasted_iota(jnp.int32, (1, LANES), 1) < c
    chains = []
    for d0, (q_ref, k_ref, v_ref, gb_ref, o_ref, rev) in enumerate(dirs):
        dp = _dir_prep(gb_ref, rev)
        for h in range(DN_HEADS):
            lb = d0 * DN_HEADS + h
            chains.append(dict(dp=dp, sl=slice(h * DN_DK, (h + 1) * DN_DK), lb=lb, lg=2 * DN_HEADS + lb,
                               q_ref=q_ref, k_ref=k_ref, v_ref=v_ref, o_ref=o_ref))
    for ch in chains:
        dp, sl, lg = ch["dp"], ch["sl"], ch["lg"]
        q, k, v = ch["q_ref"][:, sl], ch["k_ref"][:, sl], ch["v_ref"][:, sl]
        beta = dp["gb"][:, ch["lb"]:ch["lb"] + 1]
        e_col = dp["e_in"][:, lg:lg + 1]
        kb = k * beta
        k16 = k.astype(BF16)
        kk = _dot_nt(jnp.concatenate([kb, q], axis=0).astype(BF16), jnp.concatenate([k16, k16], axis=0))
        decay = jnp.exp(jnp.where(dp["incl"], dp["cs"][:, lg:lg + 1] - dp["cs_t"][lg:lg + 1, :], NEG_INF))
        a_mat = jnp.where(dp["strict"], kk[:c] * decay, 0.0)
        ch.update(p=dp["eye"] - a_mat, pw=a_mat, attn=(kk[c:] * decay)[:, :c],
                  rhs=jnp.concatenate([v * beta, kb * e_col], axis=1).astype(BF16),
                  q_dec=q * e_col, k_tail_t=(k * dp["e_tail"][:, lg:lg + 1]).T)
    n = 1
    while n < c:
        first, final = n == 1, 2 * n >= c
        for ch in chains:
            x = ch["pw"] if first else (ch["p"] if final else jnp.concatenate([ch["p"], ch["pw"]], axis=0))
            hi = x.astype(BF16)
            lo = (x - hi.astype(F32)).astype(BF16)
            mix = jnp.where(left, hi, lo)
            if final:
                w_hi = ch["pw"].astype(BF16)
                w_lo = (ch["pw"] - w_hi.astype(F32)).astype(BF16)
            else:
                w_hi, w_lo = hi[-c:], lo[-c:]
            r = _dot(jnp.concatenate([mix, mix], axis=1), jnp.concatenate([w_hi, w_hi, w_lo, w_lo], axis=0))
            if not first:
                ch["p"] = ch["p"] + r[:c]
            if not final:
                ch["pw"] = r[-c:]
        n *= 2
    for ch in chains:
        ch["uw"] = _dot(ch["p"][:, :c].astype(BF16), ch["rhs"])
    for i, ch in enumerate(chains):
        state = s_ref[i]
        sw = _dot(jnp.concatenate([ch["uw"][:, DN_DK:], ch["q_dec"]], axis=0).astype(BF16), state.astype(BF16))
        ch.update(state=state, v_new=ch["uw"][:, :DN_DK] - sw[:c], qs=sw[c:])
    for i, ch in enumerate(chains):
        lg = ch["lg"]
        fin = _dot(jnp.concatenate([ch["attn"], ch["k_tail_t"]], axis=0).astype(BF16), ch["v_new"].astype(BF16))
        ch["o_ref"][:, ch["sl"]] = ch["qs"] + fin[:c]
        s_ref[i] = ch["state"] * ch["dp"]["g_tot"][:, lg:lg + 1] + fin[c:]


def _delta_kernel(qf, kf, vf, gf, qb, kb, vb, gbb, of_ref, ob_ref, s_ref):
    @pl.when(pl.program_id(1) == 0)
    def _():
        s_ref[...] = jnp.zeros(s_ref.shape, F32)

    _delta_step([(qf, kf, vf, gf, of_ref, False), (qb, kb, vb, gbb, ob_ref, True)], s_ref)


def _mixer_d_scan(q, k, v, gb, bsz, seq_len):
    n = q.shape[0]
    nc = seq_len // DN_CHUNK
    fwd = lambda b, m: (b * nc + m, 0)
    bwd = lambda b, m: (b * nc + nc - 1 - m, 0)
    wide = lambda f: pl.BlockSpec((DN_CHUNK, BRANCH_W), f)
    small = lambda f: pl.BlockSpec((DN_CHUNK, LANES), f)
    return pl.pallas_call(
        _delta_kernel,
        grid=(bsz, nc),
        in_specs=[wide(fwd), wide(fwd), wide(fwd), small(fwd), wide(bwd), wide(bwd), wide(bwd), small(bwd)],
        out_specs=[wide(fwd), wide(bwd)],
        out_shape=[jax.ShapeDtypeStruct((n, BRANCH_W), F32)] * 2,
        scratch_shapes=[pltpu.VMEM((2 * DN_HEADS, DN_DK, DN_DK), F32)],
        compiler_params=_cparams("arbitrary", "arbitrary"),
        name="mixer_d_scan",
    )(q, k, v, gb, q, k, v, gb)


def _dout_kernel(of_ref, ob_ref, gate_ref, ng_ref, o_ref):
    o = of_ref[...] + ob_ref[...]
    gate = _silu(gate_ref[...])
    for h in range(DN_HEADS):
        sl = slice(h * DN_DK, (h + 1) * DN_DK)
        oh = o[:, sl]
        inv = lax.rsqrt(jnp.mean(oh * oh, axis=-1, keepdims=True) + EPS)
        o_ref[:, sl] = (oh * inv * ng_ref[...] * gate[:, sl]).astype(o_ref.dtype)


def _mixer_d_out(o_f, o_b, u, norm_g):
    n = o_f.shape[0]
    tt = 512
    bs = pl.BlockSpec((tt, BRANCH_W), lambda i: (i, 0))
    return pl.pallas_call(
        _dout_kernel,
        grid=(n // tt,),
        in_specs=[bs, bs, pl.BlockSpec((tt, BRANCH_W), lambda i: (i, CB_D_GATE)),
                  pl.BlockSpec((1, DN_DK), lambda i: (0, 0))],
        out_specs=bs,
        out_shape=jax.ShapeDtypeStruct((n, BRANCH_W), BF16),
        compiler_params=_cparams("arbitrary"),
        name="mixer_d_out",
    )(o_f, o_b, u, norm_g)


def _merge_kernel(h_ref, ya, yb, yc, yd, wg0, wg1, wg2, wg3, wb_ref, bg_ref, o_ref):
    acc = None
    for b, (y_ref, wg_ref) in enumerate(zip((ya, yb, yc, yd), (wg0, wg1, wg2, wg3))):
        gate = _sigmoid(_dot(h_ref[...], wg_ref[...]) + bg_ref[b:b + 1, :])
        term = gate * _dot(y_ref[...], wb_ref[b])
        acc = term if acc is None else acc + term
    o_ref[...] = acc.astype(o_ref.dtype)


def _merge(h, ys, w_gate, w_branch, b_gate):
    n, d = h.shape
    tm, tn = ROW_TILE, 512
    nj = d // tn
    ybs = pl.BlockSpec((tm, BRANCH_W), lambda i, j: (i, 0))

    def wg_spec(b):
        return pl.BlockSpec((d, tn), lambda i, j: (0, b * nj + j))

    return pl.pallas_call(
        _merge_kernel,
        grid=(n // tm, nj),
        in_specs=[
            pl.BlockSpec((tm, d), lambda i, j: (i, 0)),
            ybs, ybs, ybs, ybs,
            wg_spec(0), wg_spec(1), wg_spec(2), wg_spec(3),
            pl.BlockSpec((N_BRANCH, BRANCH_W, tn), lambda i, j: (0, 0, j)),
            pl.BlockSpec((N_BRANCH, tn), lambda i, j: (0, j)),
        ],
        out_specs=pl.BlockSpec((tm, tn), lambda i, j: (i, j)),
        out_shape=jax.ShapeDtypeStruct((n, d), BF16),
        compiler_params=_cparams("arbitrary", "arbitrary"),
        name="branch_merge",
    )(h, *ys, w_gate, w_gate, w_gate, w_gate, w_branch, b_gate)


def _outproj_kernel(m_ref, w_ref, x_ref, mod_ref, o_ref):
    o_ref[...] = x_ref[...] + mod_ref[2:3, :] * _dot(m_ref[...], w_ref[...])


def _outproj(merged, w_out, x, mods, seq_len):
    n, d = x.shape
    tm, tn = ROW_TILE, 1024
    tps = seq_len // tm
    return pl.pallas_call(
        _outproj_kernel,
        grid=(n // tm, d // tn),
        in_specs=[
            pl.BlockSpec((tm, d), lambda i, j: (i, 0)),
            pl.BlockSpec((d, tn), lambda i, j: (0, j)),
            pl.BlockSpec((tm, tn), lambda i, j: (i, j)),
            pl.BlockSpec((None, 6, tn), lambda i, j: (i // tps, 0, j)),
        ],
        out_specs=pl.BlockSpec((tm, tn), lambda i, j: (i, j)),
        out_shape=jax.ShapeDtypeStruct((n, d), F32),
        compiler_params=_cparams("arbitrary", "arbitrary"),
        name="out_proj",
    )(merged, w_out, x, mods)


def _ffn_kernel(x_ref, g_ref, mod_ref, w1_ref, w2_ref, o_ref, h_ref):
    f = pl.program_id(1)

    @pl.when(f == 0)
    def _():
        _norm_mod(x_ref, g_ref, mod_ref, 3, 4, h_ref)
        o_ref[...] = jnp.zeros(o_ref.shape, F32)

    a = jnp.maximum(_dot(h_ref[...], w1_ref[...]), 0.0)
    o_ref[...] += _dot((a * a).astype(BF16), w2_ref[...])

    @pl.when(f == pl.num_programs(1) - 1)
    def _():
        o_ref[...] = x_ref[...] + mod_ref[5:6, :] * o_ref[...]


def _ffn(x, g, mods, w1, w2, seq_len):
    n, d = x.shape
    tm, tf = ROW_TILE, 512
    tps = seq_len // tm
    return pl.pallas_call(
        _ffn_kernel,
        grid=(n // tm, D_FF // tf),
        in_specs=[
            pl.BlockSpec((tm, d), lambda i, f: (i, 0)),
            pl.BlockSpec((1, d), lambda i, f: (0, 0)),
            pl.BlockSpec((None, 6, d), lambda i, f: (i // tps, 0, 0)),
            pl.BlockSpec((d, tf), lambda i, f: (0, f)),
            pl.BlockSpec((tf, d), lambda i, f: (f, 0)),
        ],
        out_specs=pl.BlockSpec((tm, d), lambda i, f: (i, 0)),
        out_shape=jax.ShapeDtypeStruct((n, d), F32),
        scratch_shapes=[pltpu.VMEM((tm, d), BF16)],
        compiler_params=_cparams("arbitrary", "arbitrary"),
        name="ffn",
    )(x, g, mods, w1, w2)


def _final_norm_kernel(x_ref, g_ref, o_ref):
    xf = x_ref[...]
    ms = jnp.mean(xf * xf, axis=-1, keepdims=True)
    o_ref[...] = xf * lax.rsqrt(ms + EPS) * g_ref[...]


def _final_norm(x, g):
    n, d = x.shape
    tt = 256
    return pl.pallas_call(
        _final_norm_kernel,
        grid=(n // tt,),
        in_specs=[pl.BlockSpec((tt, d), lambda i: (i, 0)), pl.BlockSpec((1, d), lambda i: (0, 0))],
        out_specs=pl.BlockSpec((tt, d), lambda i: (i, 0)),
        out_shape=jax.ShapeDtypeStruct((n, d), F32),
        compiler_params=_cparams("arbitrary"),
        name="final_norm",
    )(x, g)


def _layer(x, mods, p, bsz, seq_len):
    u, us, h = _inproj(x, p["norm_mix_g"], mods, p["w_in_main"], p["w_in_small"], seq_len)
    y_a = _mixer_a(u, p["conv_a_w"], p["conv_a_b"], p["ln_a_g"], p["ln_a_b"], bsz, seq_len)
    y_b = _mixer_b(u, p["na_bias"], bsz, seq_len)
    y_c = _mixer_c(u, p["pool_w"], p["pool_scale"], bsz, seq_len)
    q, k, v, gb = _mixer_d_prep(u, us, p["dn_conv_w"], p["dn_alog_row"], p["dn_dt_row"], bsz, seq_len)
    o_f, o_b = _mixer_d_scan(q, k, v, gb, bsz, seq_len)
    y_d = _mixer_d_out(o_f, o_b, u, p["dn_norm_g"])
    merged = _merge(h, (y_a, y_b, y_c, y_d), p["w_gate"], p["w_branch"], p["b_gate"])
    x = _outproj(merged, p["w_out"], x, mods, seq_len)
    return _ffn(x, p["norm_ffn_g"], mods, p["w_ff1"], p["w_ff2"], seq_len)


def _layer_params(l, norm_mix_g, norm_ffn_g, w_in, conv_a_w, conv_a_b, ln_a_g, ln_a_b, nat_rpb, pool_w, pool_scale,
                  dn_conv_w, dn_a_log, dn_dt_bias, dn_norm_g, w_branch, w_gate, b_gate, w_out, w_ff1, w_ff2):
    d = D_MODEL
    row = lambda a: a.reshape(1, -1).astype(F32)
    small_pad = ((0, 0), (0, LANES - U_SMALL))
    lane_pad = lambda a: jnp.pad(a.reshape(1, -1).astype(F32), ((0, 0), (2 * DN_HEADS, LANES - U_SMALL)))
    return {
        "norm_mix_g": row(norm_mix_g[l]), "norm_ffn_g": row(norm_ffn_g[l]),
        "w_in_main": w_in[l][:, :U_MAIN].astype(BF16),
        "w_in_small": jnp.pad(w_in[l][:, U_MAIN:], small_pad).astype(BF16),
        "conv_a_w": jnp.broadcast_to(conv_a_w[l].astype(F32)[:, None, :], (CONV_A_K, SUBLANES, BRANCH_W)),
        "conv_a_b": row(conv_a_b[l]),
        "ln_a_g": row(ln_a_g[l]), "ln_a_b": row(ln_a_b[l]),
        "na_bias": _na_bias_table(nat_rpb[l]),
        "pool_w": pool_w[l].astype(BF16), "pool_scale": row(pool_scale[l]),
        "dn_conv_w": dn_conv_w[l].astype(F32),
        "dn_alog_row": lane_pad(dn_a_log[l]), "dn_dt_row": lane_pad(dn_dt_bias[l]),
        "dn_norm_g": row(dn_norm_g[l]),
        "w_branch": w_branch[l].astype(BF16),
        "w_gate": w_gate[l].reshape(d, N_BRANCH * d).astype(BF16),
        "b_gate": b_gate[l].astype(F32),
        "w_out": w_out[l].astype(BF16), "w_ff1": w_ff1[l].astype(BF16), "w_ff2": w_ff2[l].astype(BF16),
    }


def kernel(x_prompt, x_sample, c_prompt, c_sample, norm_mix_g, norm_ffn_g, w_ada, b_ada, w_in, conv_a_w, conv_a_b,
           ln_a_g, ln_a_b, nat_rpb, pool_w, pool_scale, dn_conv_w, dn_a_log, dn_dt_bias, dn_norm_g, w_branch, w_gate,
           b_gate, w_out, w_ff1, w_ff2, final_norm_g):
    depth = w_in.shape[0]
    layers = [_layer_params(l, norm_mix_g, norm_ffn_g, w_in, conv_a_w, conv_a_b, ln_a_g, ln_a_b, nat_rpb, pool_w,
                            pool_scale, dn_conv_w, dn_a_log, dn_dt_bias, dn_norm_g, w_branch, w_gate, b_gate, w_out,
                            w_ff1, w_ff2) for l in range(depth)]
    n_prompt = c_prompt.shape[0]
    mods = _ada(jnp.concatenate([c_prompt, c_sample], axis=0), w_ada, b_ada)
    fg = final_norm_g.reshape(1, -1).astype(F32)
    outs = []
    for x, lo, hi in ((x_prompt, 0, n_prompt), (x_sample, n_prompt, mods.shape[1])):
        bsz, seq_len, d = x.shape
        xt = x.reshape(bsz * seq_len, d)
        for l in range(depth):
            xt = _layer(xt, mods[l, lo:hi], layers[l], bsz, seq_len)
        outs.append(_final_norm(xt, fg).reshape(bsz, seq_len, d))
    return tuple(outs)
```

```python
import functools

import jax
import jax.numpy as jnp
import numpy as np
from jax import lax
from jax.experimental import pallas as pl
from jax.experimental.pallas import tpu as pltpu

F32 = jnp.float32
BF16 = jnp.bfloat16

D_MODEL = 2048
N_BRANCH = 4
BRANCH_W = D_MODEL // 4
D_FF = 4 * D_MODEL
EPS = 1e-6
NEG_INF = -1e30

CONV_A_K = 31
GRID_W = 64
NA_HEADS = 8
NA_HEAD_DIM = BRANCH_W // NA_HEADS
NA_KH = 8
NA_KW = 16
POOL_WINDOWS = (2, 4, 8, 16)
POOL_GW = BRANCH_W // len(POOL_WINDOWS)
DN_HEADS = 4
DN_DK = BRANCH_W // DN_HEADS
DN_CONV_K = 5
DN_CHUNK = 64

U_MAIN = 10 * BRANCH_W
U_SMALL = 4 * DN_HEADS
LANES = 128
SUBLANES = 8
CB_A_VAL, CB_A_GATE, CB_B_Q, CB_B_K, CB_B_V, CB_C, CB_D_Q, CB_D_K, CB_D_V, CB_D_GATE = range(10)

VMEM_LIMIT = 56 * 1024 * 1024
ROW_TILE = 1024
NORM_ROWS = 16
NORM_UNROLL = 8
HALO_TT = 256
HALO_A = 16
HALO_D = 8
NA_ROWS = 8
DN_SEQ_PER_STEP = 2


def _cparams(*sem):
    return pltpu.CompilerParams(dimension_semantics=sem, vmem_limit_bytes=VMEM_LIMIT)


def _sigmoid(x):
    return 1.0 / (1.0 + jnp.exp(-x))


def _silu(x):
    return x * _sigmoid(x)


def _dot(a, b):
    return jnp.dot(a, b, preferred_element_type=F32)


def _dot_nt(a, b):
    return lax.dot_general(a, b, (((1,), (1,)), ((), ())), preferred_element_type=F32)


def _ada_kernel(c_ref, w_ref, b_ref, o_ref):
    o_ref[...] = _dot(c_ref[...].astype(BF16), w_ref[...].astype(BF16)) + b_ref[...]


def _ada(c_all, w_ada, b_ada):
    nl, d, n6 = w_ada.shape
    nb = c_all.shape[0]
    tn = 512
    out = pl.pallas_call(
        _ada_kernel,
        grid=(nl, n6 // tn),
        in_specs=[
            pl.BlockSpec((nb, d), lambda l, j: (0, 0)),
            pl.BlockSpec((None, d, tn), lambda l, j: (l, 0, j)),
            pl.BlockSpec((None, 1, tn), lambda l, j: (l, 0, j)),
        ],
        out_specs=pl.BlockSpec((None, nb, tn), lambda l, j: (l, 0, j)),
        out_shape=jax.ShapeDtypeStruct((nl, nb, n6), F32),
        compiler_params=_cparams("arbitrary", "arbitrary"),
        name="ada_mod",
    )(c_all, w_ada, b_ada.reshape(nl, 1, n6))
    return out.reshape(nl, nb, 6, d)


def _norm_mod(x_ref, g_ref, mod_ref, shift_idx, scale_idx, h_ref):
    gain = g_ref[...] * (1.0 + mod_ref[scale_idx:scale_idx + 1, :])
    shift = mod_ref[shift_idx:shift_idx + 1, :]

    def body(c, carry):
        r0 = pl.multiple_of(c * NORM_ROWS, NORM_ROWS)
        xf = x_ref[pl.ds(r0, NORM_ROWS), :]
        ms = jnp.mean(xf * xf, axis=-1, keepdims=True)
        h_ref[pl.ds(r0, NORM_ROWS), :] = (xf * lax.rsqrt(ms + EPS) * gain + shift).astype(h_ref.dtype)
        return carry

    lax.fori_loop(0, x_ref.shape[0] // NORM_ROWS, body, 0, unroll=NORM_UNROLL)


def _inproj_kernel(x_ref, g_ref, mod_ref, w_ref, ws_ref, u_ref, us_ref, h_ref):
    @pl.when(pl.program_id(1) == 0)
    def _():
        _norm_mod(x_ref, g_ref, mod_ref, 0, 1, h_ref)
        us_ref[...] = _dot(h_ref[...], ws_ref[...])

    u_ref[...] = _dot(h_ref[...], w_ref[...])


def _inproj(x, g, mods, w_main, w_small, seq_len):
    n, d = x.shape
    tm, tn = ROW_TILE, 1024
    tps = seq_len // tm
    return pl.pallas_call(
        _inproj_kernel,
        grid=(n // tm, U_MAIN // tn),
        in_specs=[
            pl.BlockSpec((tm, d), lambda i, j: (i, 0)),
            pl.BlockSpec((1, d), lambda i, j: (0, 0)),
            pl.BlockSpec((None, 6, d), lambda i, j: (i // tps, 0, 0)),
            pl.BlockSpec((d, tn), lambda i, j: (0, j)),
            pl.BlockSpec((d, LANES), lambda i, j: (0, 0)),
        ],
        out_specs=[
            pl.BlockSpec((tm, tn), lambda i, j: (i, j)),
            pl.BlockSpec((tm, LANES), lambda i, j: (i, 0)),
            pl.BlockSpec((tm, d), lambda i, j: (i, 0)),
        ],
        out_shape=[jax.ShapeDtypeStruct((n, U_MAIN), F32), jax.ShapeDtypeStruct((n, LANES), F32),
                   jax.ShapeDtypeStruct((n, d), BF16)],
        compiler_params=_cparams("arbitrary", "arbitrary"),
        name="in_proj",
    )(x, g, mods, w_main, w_small)


def _halo_specs(col_block, tt, halo, nt, bsz):
    per = tt // halo
    last = bsz * nt * per - 1

    def main(b, t):
        return (b * nt + t, col_block)

    def prev(b, t):
        return (jnp.maximum((b * nt + t) * per - 1, 0), col_block)

    def nxt(b, t):
        return (jnp.minimum((b * nt + t + 1) * per, last), col_block)

    return [pl.BlockSpec((tt, BRANCH_W), main), pl.BlockSpec((halo, BRANCH_W), prev),
            pl.BlockSpec((halo, BRANCH_W), nxt)]


def _fill_ext(ext_ref, main, prev, nxt, halo, tt):
    t = pl.program_id(1)
    ext_ref[halo:halo + tt, :] = main
    ext_ref[0:halo, :] = jnp.where(t > 0, prev, 0.0)
    ext_ref[halo + tt:2 * halo + tt, :] = jnp.where(t < pl.num_programs(1) - 1, nxt, 0.0)


def _conva_kernel(v_ref, pv_ref, nv_ref, g_ref, pg_ref, ng_ref, w_ref, b_ref, lg_ref, lb_ref, o_ref, a_ref, sh_ref):
    tt = o_ref.shape[0]
    _fill_ext(a_ref, v_ref[...] * _sigmoid(g_ref[...]), pv_ref[...] * _sigmoid(pg_ref[...]),
              nv_ref[...] * _sigmoid(ng_ref[...]), HALO_A, tt)
    rows = 32
    off = HALO_A - CONV_A_K // 2
    span = sh_ref.shape[1]
    for ph in range(1, SUBLANES):
        sh_ref[ph - 1] = a_ref[ph:ph + span, :]
    for c in range(tt // rows):
        parts = [jnp.zeros((SUBLANES, BRANCH_W), F32) + b_ref[...] for _ in range(rows // SUBLANES)]
        for k in range(CONV_A_K):
            ph, base = (k + off) % SUBLANES, c * rows + (k + off) // SUBLANES * SUBLANES
            wk = w_ref[k]
            for i in range(len(parts)):
                r0 = base + i * SUBLANES
                tap = a_ref[r0:r0 + SUBLANES, :] if ph == 0 else sh_ref[ph - 1, r0:r0 + SUBLANES, :]
                parts[i] = parts[i] + tap * wk
        acc = jnp.concatenate(parts, axis=0)
        mu = jnp.mean(acc, axis=-1, keepdims=True)
        dlt = acc - mu
        var = jnp.mean(dlt * dlt, axis=-1, keepdims=True)
        y = dlt * lax.rsqrt(var + EPS) * lg_ref[...] + lb_ref[...]
        o_ref[c * rows:(c + 1) * rows, :] = _silu(y).astype(o_ref.dtype)


def _mixer_a(u, conv_w, conv_b, ln_g, ln_b, bsz, seq_len):
    n = u.shape[0]
    tt = HALO_TT
    nt = seq_len // tt
    row = lambda b, t: (0, 0)
    return pl.pallas_call(
        _conva_kernel,
        grid=(bsz, nt),
        in_specs=_halo_specs(CB_A_VAL, tt, HALO_A, nt, bsz) + _halo_specs(CB_A_GATE, tt, HALO_A, nt, bsz) + [
            pl.BlockSpec((CONV_A_K, SUBLANES, BRANCH_W), lambda b, t: (0, 0, 0)),
            pl.BlockSpec((1, BRANCH_W), row), pl.BlockSpec((1, BRANCH_W), row), pl.BlockSpec((1, BRANCH_W), row),
        ],
        out_specs=pl.BlockSpec((tt, BRANCH_W), lambda b, t: (b * nt + t, 0)),
        out_shape=jax.ShapeDtypeStruct((n, BRANCH_W), BF16),
        scratch_shapes=[pltpu.VMEM((tt + 2 * HALO_A, BRANCH_W), F32),
                        pltpu.VMEM((SUBLANES - 1, tt + 2 * HALO_A - SUBLANES, BRANCH_W), F32)],
        compiler_params=_cparams("arbitrary", "arbitrary"),
        name="mixer_a_conv",
    )(u, u, u, u, u, u, conv_w, conv_b, ln_g, ln_b)


def _pool_kernel(x_ref, px_ref, nx_ref, pw_ref, ps_ref, o_ref, a_ref, *, seq_len):
    tt = o_ref.shape[0]
    _fill_ext(a_ref, x_ref[...], px_ref[...], nx_ref[...], HALO_A, tt)
    pos = pl.program_id(1) * tt + lax.broadcasted_iota(jnp.int32, (tt, 1), 0)
    for gi, win in enumerate(POOL_WINDOWS):
        half = win // 2
        lo, hi = gi * POOL_GW, (gi + 1) * POOL_GW
        acc = jnp.zeros((tt, POOL_GW), F32)
        for j in range(-half, half):
            acc = acc + a_ref[HALO_A + j:HALO_A + j + tt, lo:hi]
        cnt = (jnp.minimum(pos + half, seq_len) - jnp.maximum(pos - half, 0)).astype(F32)
        pooled = acc / cnt - x_ref[:, lo:hi]
        mixed = _dot(pooled.astype(BF16), pw_ref[gi])
        o_ref[:, lo:hi] = (mixed * ps_ref[:, lo:hi]).astype(o_ref.dtype)


def _mixer_c(u, pool_w, pool_scale, bsz, seq_len):
    n = u.shape[0]
    tt = HALO_TT
    nt = seq_len // tt
    return pl.pallas_call(
        functools.partial(_pool_kernel, seq_len=seq_len),
        grid=(bsz, nt),
        in_specs=_halo_specs(CB_C, tt, HALO_A, nt, bsz) + [
            pl.BlockSpec((len(POOL_WINDOWS), POOL_GW, POOL_GW), lambda b, t: (0, 0, 0)),
            pl.BlockSpec((1, BRANCH_W), lambda b, t: (0, 0)),
        ],
        out_specs=pl.BlockSpec((tt, BRANCH_W), lambda b, t: (b * nt + t, 0)),
        out_shape=jax.ShapeDtypeStruct((n, BRANCH_W), BF16),
        scratch_shapes=[pltpu.VMEM((tt + 2 * HALO_A, BRANCH_W), F32)],
        compiler_params=_cparams("arbitrary", "arbitrary"),
        name="mixer_c_pool",
    )(u, u, u, pool_w, pool_scale)


def _na_bias_table(rpb):
    c = np.arange(GRID_W)
    col_start = np.clip(c - NA_KW // 2, 0, GRID_W - NA_KW)
    col_ok = (c[None, :] >= col_start[:, None]) & (c[None, :] < col_start[:, None] + NA_KW)
    dc = np.clip(c[None, :] - c[:, None] + (NA_KW - 1), 0, 2 * NA_KW - 2)
    case = np.arange(NA_KH)[:, None] + np.arange(NA_KH)[None, :]
    tab = rpb.astype(F32)[:, case][:, :, :, dc]
    tab = jnp.where(col_ok[None, None, None], tab, NEG_INF)
    tab = jnp.transpose(tab, (1, 0, 3, 2, 4))
    return tab.reshape(NA_KH, NA_HEADS // 2, 2 * GRID_W, NA_KH * GRID_W)


def _na_kernel(q_ref, kp_ref, kc_ref, kn_ref, vp_ref, vc_ref, vn_ref, bt_ref, o_ref, k_ext, v_ext, *, n_rows):
    blk = NA_ROWS * GRID_W
    win = NA_KH * GRID_W
    j = pl.program_id(1)
    for i, (kr, vr) in enumerate(((kp_ref, vp_ref), (kc_ref, vc_ref), (kn_ref, vn_ref))):
        k_ext[i * blk:(i + 1) * blk, :] = kr[...].astype(BF16)
        v_ext[i * blk:(i + 1) * blk, :] = vr[...].astype(BF16)
    lane = lax.broadcasted_iota(jnp.int32, (1, LANES), 1)
    first_head = lane < NA_HEAD_DIM

    def row_body(ri, carry):
        r = j * NA_ROWS + ri
        rs = jnp.clip(r - NA_KH // 2, 0, n_rows - NA_KH)
        case = rs - r + (NA_KH - 1)
        start = pl.multiple_of((rs - j * NA_ROWS + NA_ROWS) * GRID_W, GRID_W)
        q0 = pl.multiple_of(ri * GRID_W, GRID_W)
        pairs = range(NA_HEADS // 2)
        lanes = [slice(hp * LANES, (hp + 1) * LANES) for hp in pairs]
        scores = []
        for hp in pairs:
            qp = q_ref[pl.ds(q0, GRID_W), lanes[hp]] * (NA_HEAD_DIM ** -0.5)
            qs = jnp.concatenate([jnp.where(first_head, qp, 0.0), jnp.where(first_head, 0.0, qp)], axis=0)
            scores.append(_dot_nt(qs.astype(BF16), k_ext[pl.ds(start, win), lanes[hp]]) + bt_ref[case, hp])
        probs, norms = [], []
        for s in scores:
            p = jnp.exp(s - jnp.max(s, axis=-1, keepdims=True))
            probs.append(p.astype(BF16))
            norms.append(jnp.sum(p, axis=-1, keepdims=True))
        for hp in pairs:
            pv = _dot(probs[hp], v_ext[pl.ds(start, win), lanes[hp]]) / norms[hp]
            o_ref[pl.ds(q0, GRID_W), lanes[hp]] = jnp.where(first_head, pv[:GRID_W], pv[GRID_W:]).astype(o_ref.dtype)
        return carry

    lax.fori_loop(0, NA_ROWS, row_body, 0, unroll=2)


def _mixer_b(u, bias_tab, bsz, seq_len):
    n = u.shape[0]
    blk = NA_ROWS * GRID_W
    n_rows = seq_len // GRID_W
    ng = seq_len // blk

    def cur(col):
        return lambda b, j: (b * ng + j, col)

    def prev(col):
        return lambda b, j: (b * ng + jnp.maximum(j - 1, 0), col)

    def nxt(col):
        return lambda b, j: (b * ng + jnp.minimum(j + 1, ng - 1), col)

    bs = lambda f: pl.BlockSpec((blk, BRANCH_W), f)
    return pl.pallas_call(
        functools.partial(_na_kernel, n_rows=n_rows),
        grid=(bsz, ng),
        in_specs=[bs(cur(CB_B_Q)), bs(prev(CB_B_K)), bs(cur(CB_B_K)), bs(nxt(CB_B_K)),
                  bs(prev(CB_B_V)), bs(cur(CB_B_V)), bs(nxt(CB_B_V)),
                  pl.BlockSpec(bias_tab.shape, lambda b, j: (0, 0, 0, 0))],
        out_specs=pl.BlockSpec((blk, BRANCH_W), lambda b, j: (b * ng + j, 0)),
        out_shape=jax.ShapeDtypeStruct((n, BRANCH_W), BF16),
        scratch_shapes=[pltpu.VMEM((3 * blk, BRANCH_W), BF16), pltpu.VMEM((3 * blk, BRANCH_W), BF16)],
        compiler_params=_cparams("arbitrary", "arbitrary"),
        name="mixer_b_na",
    )(u, u, u, u, u, u, u, bias_tab)


def _dprep_kernel(q_ref, pq_ref, nq_ref, k_ref, pk_ref, nk_ref, v_ref, pv_ref, nv_ref, us_ref,
                  w_ref, alog_ref, dt_ref, qo_ref, ko_ref, vo_ref, gb_ref, e_ref):
    tt = qo_ref.shape[0]
    off = HALO_D - DN_CONV_K // 2
    srcs = ((q_ref, pq_ref, nq_ref, qo_ref), (k_ref, pk_ref, nk_ref, ko_ref), (v_ref, pv_ref, nv_ref, vo_ref))
    for ci, (m_ref, p_ref, n_ref, out_ref) in enumerate(srcs):
        _fill_ext(e_ref, m_ref[...], p_ref[...], n_ref[...], HALO_D, tt)
        acc = jnp.zeros((tt, BRANCH_W), F32)
        for k in range(DN_CONV_K):
            acc = acc + e_ref[off + k:off + k + tt, :] * w_ref[k:k + 1, ci * BRANCH_W:(ci + 1) * BRANCH_W]
        y = _silu(acc)
        if ci == 2:
            out_ref[...] = y
        else:
            post = DN_DK ** -0.5 if ci == 0 else 1.0
            for h in range(DN_HEADS):
                yh = y[:, h * DN_DK:(h + 1) * DN_DK]
                inv = lax.rsqrt(jnp.sum(yh * yh, axis=-1, keepdims=True) + EPS)
                out_ref[:, h * DN_DK:(h + 1) * DN_DK] = yh * inv * post
    us = us_ref[...]
    lane = lax.broadcasted_iota(jnp.int32, us.shape, 1)
    z = us + dt_ref[...]
    softplus = jnp.maximum(z, 0.0) + jnp.log1p(jnp.exp(-jnp.abs(z)))
    gb_ref[...] = jnp.where(lane < 2 * DN_HEADS, _sigmoid(us),
                            jnp.where(lane < U_SMALL, -jnp.exp(alog_ref[...]) * softplus, 0.0))


def _mixer_d_prep(u, us, conv_w, alog_row, dt_row, bsz, seq_len):
    n = u.shape[0]
    tt = HALO_TT
    nt = seq_len // tt
    out_bs = pl.BlockSpec((tt, BRANCH_W), lambda b, t: (b * nt + t, 0))
    sm_bs = pl.BlockSpec((tt, LANES), lambda b, t: (b * nt + t, 0))
    row = lambda b, t: (0, 0)
    return pl.pallas_call(
        _dprep_kernel,
        grid=(bsz, nt),
        in_specs=(_halo_specs(CB_D_Q, tt, HALO_D, nt, bsz) + _halo_specs(CB_D_K, tt, HALO_D, nt, bsz)
                  + _halo_specs(CB_D_V, tt, HALO_D, nt, bsz)
                  + [sm_bs, pl.BlockSpec((DN_CONV_K, 3 * BRANCH_W), row),
                     pl.BlockSpec((1, LANES), row), pl.BlockSpec((1, LANES), row)]),
        out_specs=[out_bs, out_bs, out_bs, sm_bs],
        out_shape=[jax.ShapeDtypeStruct((n, BRANCH_W), F32)] * 3 + [jax.ShapeDtypeStruct((n, LANES), F32)],
        scratch_shapes=[pltpu.VMEM((tt + 2 * HALO_D, BRANCH_W), F32)],
        compiler_params=_cparams("arbitrary", "arbitrary"),
        name="mixer_d_prep",
    )(u, u, u, u, u, u, u, u, u, us, conv_w, alog_row, dt_row)


def _dir_prep(gb_ref, rev):
    c = DN_CHUNK
    ri = lax.broadcasted_iota(jnp.int32, (c, LANES), 0)
    ci = lax.broadcasted_iota(jnp.int32, (c, LANES), 1) % c
    incl = (ri <= ci) if rev else (ri >= ci)
    strict = (ri < ci) if rev else (ri > ci)
    gb = gb_ref[...]
    g1 = gb.astype(BF16)
    r1 = gb - g1.astype(F32)
    g2 = r1.astype(BF16)
    g3 = (r1 - g2.astype(F32)).astype(BF16)
    tri = incl[:, :c].astype(BF16)
    cs = _dot(jnp.concatenate([tri, tri, tri], axis=1), jnp.concatenate([g1, g2, g3], axis=0))
    last = 0 if rev else c - 1
    cs_last = cs[last:last + 1, :]
    return dict(incl=incl, strict=strict, eye=(ri == ci).astype(F32), gb=gb, cs=cs,
                cs_t=jnp.concatenate([cs, cs], axis=0).T, e_in=jnp.exp(cs), e_tail=jnp.exp(cs_last - cs),
                g_tot=jnp.exp(cs_last))


def _delta_step(dirs, s_ref):
    c = DN_CHUNK
    left = lax.broadcasted_iota(jnp.int32, (1, LANES), 1) < c
    chains = []
    for q_ref, k_ref, v_ref, gb_ref, o_ref, rev in dirs:
        dp = _dir_prep(gb_ref, rev)
        for h in range(DN_HEADS):
            lb = int(rev) * DN_HEADS + h
            chains.append(dict(dp=dp, sl=slice(h * DN_DK, (h + 1) * DN_DK), lb=lb, lg=2 * DN_HEADS + lb,
                               q_ref=q_ref, k_ref=k_ref, v_ref=v_ref, o_ref=o_ref))
    for ch in chains:
        dp, sl, lg = ch["dp"], ch["sl"], ch["lg"]
        q, k, v = ch["q_ref"][:, sl], ch["k_ref"][:, sl], ch["v_ref"][:, sl]
        beta = dp["gb"][:, ch["lb"]:ch["lb"] + 1]
        e_col = dp["e_in"][:, lg:lg + 1]
        kb = k * beta
        k16 = k.astype(BF16)
        kk = _dot_nt(jnp.concatenate([kb, q], axis=0).astype(BF16), jnp.concatenate([k16, k16], axis=0))
        decay = jnp.exp(jnp.where(dp["incl"], dp["cs"][:, lg:lg + 1] - dp["cs_t"][lg:lg + 1, :], NEG_INF))
        a_mat = jnp.where(dp["strict"], kk[:c] * decay, 0.0)
        ch.update(w=jnp.where(left, dp["eye"], -a_mat), attn=(kk[c:] * decay)[:, :c],
                  rhs=jnp.concatenate([v * beta, kb * e_col], axis=1).astype(BF16),
                  q_dec=q * e_col, k_tail_t=(k * dp["e_tail"][:, lg:lg + 1]).T)
    n = 1
    while n < c:
        for ch in chains:
            w = ch["w"]
            hi = w.astype(BF16)
            lo = (w - hi.astype(F32)).astype(BF16)
            q_mix = jnp.where(left, pltpu.roll(w, c, 1).astype(BF16), lo)
            r = _dot(jnp.concatenate([q_mix, q_mix], axis=1), jnp.concatenate([hi, hi, lo, lo], axis=0))
            ch["w"] = jnp.where(left, w + r, r)
        n *= 2
    for ch in chains:
        ch["uw"] = _dot(ch["w"][:, :c].astype(BF16), ch["rhs"])
    for i, ch in enumerate(chains):
        state = s_ref[i]
        sw = _dot(jnp.concatenate([ch["uw"][:, DN_DK:], ch["q_dec"]], axis=0).astype(BF16), state.astype(BF16))
        ch.update(state=state, v_new=ch["uw"][:, :DN_DK] - sw[:c], qs=sw[c:])
    for i, ch in enumerate(chains):
        lg = ch["lg"]
        fin = _dot(jnp.concatenate([ch["attn"], ch["k_tail_t"]], axis=0).astype(BF16), ch["v_new"].astype(BF16))
        ch["o_ref"][:, ch["sl"]] = ch["qs"] + fin[:c]
        s_ref[i] = ch["state"] * ch["dp"]["g_tot"][:, lg:lg + 1] + fin[c:]


def _delta_kernel(qf, kf, vf, gf, qb, kb, vb, gbb, of_ref, ob_ref, s_ref):
    @pl.when(pl.program_id(1) == 0)
    def _():
        s_ref[...] = jnp.zeros(s_ref.shape, F32)

    dirs = []
    for s in range(qf.shape[0]):
        dirs += [(qf.at[s], kf.at[s], vf.at[s], gf.at[s], of_ref.at[s], False),
                 (qb.at[s], kb.at[s], vb.at[s], gbb.at[s], ob_ref.at[s], True)]
    _delta_step(dirs, s_ref)


def _mixer_d_scan(q, k, v, gb, bsz, seq_len):
    n = q.shape[0]
    nc = seq_len // DN_CHUNK
    n_seq = DN_SEQ_PER_STEP
    grouped = lambda a: a.reshape(bsz // n_seq, n_seq, seq_len, a.shape[-1])
    fwd = lambda b, m: (b, 0, m, 0)
    bwd = lambda b, m: (b, 0, nc - 1 - m, 0)
    wide = lambda f: pl.BlockSpec((None, n_seq, DN_CHUNK, BRANCH_W), f)
    small = lambda f: pl.BlockSpec((None, n_seq, DN_CHUNK, LANES), f)
    q, k, v, gb = grouped(q), grouped(k), grouped(v), grouped(gb)
    o_f, o_b = pl.pallas_call(
        _delta_kernel,
        grid=(bsz // n_seq, nc),
        in_specs=[wide(fwd), wide(fwd), wide(fwd), small(fwd), wide(bwd), wide(bwd), wide(bwd), small(bwd)],
        out_specs=[wide(fwd), wide(bwd)],
        out_shape=[jax.ShapeDtypeStruct(q.shape, F32)] * 2,
        scratch_shapes=[pltpu.VMEM((2 * DN_HEADS * n_seq, DN_DK, DN_DK), F32)],
        compiler_params=_cparams("arbitrary", "arbitrary"),
        name="mixer_d_scan",
    )(q, k, v, gb, q, k, v, gb)
    return o_f.reshape(n, BRANCH_W), o_b.reshape(n, BRANCH_W)


def _dout_kernel(of_ref, ob_ref, gate_ref, ng_ref, o_ref):
    o = of_ref[...] + ob_ref[...]
    gate = _silu(gate_ref[...])
    for h in range(DN_HEADS):
        sl = slice(h * DN_DK, (h + 1) * DN_DK)
        oh = o[:, sl]
        inv = lax.rsqrt(jnp.mean(oh * oh, axis=-1, keepdims=True) + EPS)
        o_ref[:, sl] = (oh * inv * ng_ref[...] * gate[:, sl]).astype(o_ref.dtype)


def _mixer_d_out(o_f, o_b, u, norm_g):
    n = o_f.shape[0]
    tt = 512
    bs = pl.BlockSpec((tt, BRANCH_W), lambda i: (i, 0))
    return pl.pallas_call(
        _dout_kernel,
        grid=(n // tt,),
        in_specs=[bs, bs, pl.BlockSpec((tt, BRANCH_W), lambda i: (i, CB_D_GATE)),
                  pl.BlockSpec((1, DN_DK), lambda i: (0, 0))],
        out_specs=bs,
        out_shape=jax.ShapeDtypeStruct((n, BRANCH_W), BF16),
        compiler_params=_cparams("arbitrary"),
        name="mixer_d_out",
    )(o_f, o_b, u, norm_g)


def _merge_kernel(h_ref, ya, yb, yc, yd, wg0, wg1, wg2, wg3, wb_ref, bg_ref, o_ref):
    acc = None
    for b, (y_ref, wg_ref) in enumerate(zip((ya, yb, yc, yd), (wg0, wg1, wg2, wg3))):
        gate = _sigmoid(_dot(h_ref[...], wg_ref[...]) + bg_ref[b:b + 1, :])
        term = gate * _dot(y_ref[...], wb_ref[b])
        acc = term if acc is None else acc + term
    o_ref[...] = acc.astype(o_ref.dtype)


def _merge(h, ys, w_gate, w_branch, b_gate):
    n, d = h.shape
    tm, tn = ROW_TILE, 512
    nj = d // tn
    ybs = pl.BlockSpec((tm, BRANCH_W), lambda i, j: (i, 0))

    def wg_spec(b):
        return pl.BlockSpec((d, tn), lambda i, j: (0, b * nj + j))

    return pl.pallas_call(
        _merge_kernel,
        grid=(n // tm, nj),
        in_specs=[
            pl.BlockSpec((tm, d), lambda i, j: (i, 0)),
            ybs, ybs, ybs, ybs,
            wg_spec(0), wg_spec(1), wg_spec(2), wg_spec(3),
            pl.BlockSpec((N_BRANCH, BRANCH_W, tn), lambda i, j: (0, 0, j)),
            pl.BlockSpec((N_BRANCH, tn), lambda i, j: (0, j)),
        ],
        out_specs=pl.BlockSpec((tm, tn), lambda i, j: (i, j)),
        out_shape=jax.ShapeDtypeStruct((n, d), BF16),
        compiler_params=_cparams("arbitrary", "arbitrary"),
        name="branch_merge",
    )(h, *ys, w_gate, w_gate, w_gate, w_gate, w_branch, b_gate)


def _outproj_kernel(m_ref, w_ref, x_ref, mod_ref, o_ref):
    o_ref[...] = x_ref[...] + mod_ref[2:3, :] * _dot(m_ref[...], w_ref[...])


def _outproj(merged, w_out, x, mods, seq_len):
    n, d = x.shape
    tm, tn = ROW_TILE, 1024
    tps = seq_len // tm
    return pl.pallas_call(
        _outproj_kernel,
        grid=(n // tm, d // tn),
        in_specs=[
            pl.BlockSpec((tm, d), lambda i, j: (i, 0)),
            pl.BlockSpec((d, tn), lambda i, j: (0, j)),
            pl.BlockSpec((tm, tn), lambda i, j: (i, j)),
            pl.BlockSpec((None, 6, tn), lambda i, j: (i // tps, 0, j)),
        ],
        out_specs=pl.BlockSpec((tm, tn), lambda i, j: (i, j)),
        out_shape=jax.ShapeDtypeStruct((n, d), F32),
        compiler_params=_cparams("arbitrary", "arbitrary"),
        name="out_proj",
    )(merged, w_out, x, mods)


def _ffn_kernel(x_ref, g_ref, mod_ref, w1_ref, w2_ref, o_ref, h_ref):
    f = pl.program_id(1)

    @pl.when(f == 0)
    def _():
        _norm_mod(x_ref, g_ref, mod_ref, 3, 4, h_ref)
        o_ref[...] = x_ref[...]

    a = jnp.maximum(_dot(h_ref[...], w1_ref[...]), 0.0)
    o_ref[...] += mod_ref[5:6, :] * _dot((a * a).astype(BF16), w2_ref[...])


def _ffn(x, g, mods, w1, w2, seq_len):
    n, d = x.shape
    tm, tf = ROW_TILE, 512
    tps = seq_len // tm
    return pl.pallas_call(
        _ffn_kernel,
        grid=(n // tm, D_FF // tf),
        in_specs=[
            pl.BlockSpec((tm, d), lambda i, f: (i, 0)),
            pl.BlockSpec((1, d), lambda i, f: (0, 0)),
            pl.BlockSpec((None, 6, d), lambda i, f: (i // tps, 0, 0)),
            pl.BlockSpec((d, tf), lambda i, f: (0, f)),
            pl.BlockSpec((tf, d), lambda i, f: (f, 0)),
        ],
        out_specs=pl.BlockSpec((tm, d), lambda i, f: (i, 0)),
        out_shape=jax.ShapeDtypeStruct((n, d), F32),
        scratch_shapes=[pltpu.VMEM((tm, d), BF16)],
        compiler_params=_cparams("arbitrary", "arbitrary"),
        name="ffn",
    )(x, g, mods, w1, w2)


def _final_norm_kernel(x_ref, g_ref, o_ref):
    xf = x_ref[...]
    ms = jnp.mean(xf * xf, axis=-1, keepdims=True)
    o_ref[...] = xf * lax.rsqrt(ms + EPS) * g_ref[...]


def _final_norm(x, g):
    n, d = x.shape
    tt = 256
    return pl.pallas_call(
        _final_norm_kernel,
        grid=(n // tt,),
        in_specs=[pl.BlockSpec((tt, d), lambda i: (i, 0)), pl.BlockSpec((1, d), lambda i: (0, 0))],
        out_specs=pl.BlockSpec((tt, d), lambda i: (i, 0)),
        out_shape=jax.ShapeDtypeStruct((n, d), F32),
        compiler_params=_cparams("arbitrary"),
        name="final_norm",
    )(x, g)


def _layer(x, mods, p, bsz, seq_len):
    u, us, h = _inproj(x, p["norm_mix_g"], mods, p["w_in_main"], p["w_in_small"], seq_len)
    y_a = _mixer_a(u, p["conv_a_w"], p["conv_a_b"], p["ln_a_g"], p["ln_a_b"], bsz, seq_len)
    y_b = _mixer_b(u, p["na_bias"], bsz, seq_len)
    y_c = _mixer_c(u, p["pool_w"], p["pool_scale"], bsz, seq_len)
    q, k, v, gb = _mixer_d_prep(u, us, p["dn_conv_w"], p["dn_alog_row"], p["dn_dt_row"], bsz, seq_len)
    o_f, o_b = _mixer_d_scan(q, k, v, gb, bsz, seq_len)
    y_d = _mixer_d_out(o_f, o_b, u, p["dn_norm_g"])
    merged = _merge(h, (y_a, y_b, y_c, y_d), p["w_gate"], p["w_branch"], p["b_gate"])
    x = _outproj(merged, p["w_out"], x, mods, seq_len)
    return _ffn(x, p["norm_ffn_g"], mods, p["w_ff1"], p["w_ff2"], seq_len)


def _layer_params(l, norm_mix_g, norm_ffn_g, w_in, conv_a_w, conv_a_b, ln_a_g, ln_a_b, nat_rpb, pool_w, pool_scale,
                  dn_conv_w, dn_a_log, dn_dt_bias, dn_norm_g, w_branch, w_gate, b_gate, w_out, w_ff1, w_ff2):
    d = D_MODEL
    row = lambda a: a.reshape(1, -1).astype(F32)
    small_pad = ((0, 0), (0, LANES - U_SMALL))
    lane_pad = lambda a: jnp.pad(a.reshape(1, -1).astype(F32), ((0, 0), (2 * DN_HEADS, LANES - U_SMALL)))
    return {
        "norm_mix_g": row(norm_mix_g[l]), "norm_ffn_g": row(norm_ffn_g[l]),
        "w_in_main": w_in[l][:, :U_MAIN].astype(BF16),
        "w_in_small": jnp.pad(w_in[l][:, U_MAIN:], small_pad).astype(BF16),
        "conv_a_w": jnp.broadcast_to(conv_a_w[l].astype(F32)[:, None, :], (CONV_A_K, SUBLANES, BRANCH_W)),
        "conv_a_b": row(conv_a_b[l]),
        "ln_a_g": row(ln_a_g[l]), "ln_a_b": row(ln_a_b[l]),
        "na_bias": _na_bias_table(nat_rpb[l]),
        "pool_w": pool_w[l].astype(BF16), "pool_scale": row(pool_scale[l]),
        "dn_conv_w": dn_conv_w[l].astype(F32),
        "dn_alog_row": lane_pad(dn_a_log[l]), "dn_dt_row": lane_pad(dn_dt_bias[l]),
        "dn_norm_g": row(dn_norm_g[l]),
        "w_branch": w_branch[l].astype(BF16),
        "w_gate": w_gate[l].reshape(d, N_BRANCH * d).astype(BF16),
        "b_gate": b_gate[l].astype(F32),
        "w_out": w_out[l].astype(BF16), "w_ff1": w_ff1[l].astype(BF16), "w_ff2": w_ff2[l].astype(BF16),
    }


def kernel(x_prompt, x_sample, c_prompt, c_sample, norm_mix_g, norm_ffn_g, w_ada, b_ada, w_in, conv_a_w, conv_a_b,
           ln_a_g, ln_a_b, nat_rpb, pool_w, pool_scale, dn_conv_w, dn_a_log, dn_dt_bias, dn_norm_g, w_branch, w_gate,
           b_gate, w_out, w_ff1, w_ff2, final_norm_g):
    depth = w_in.shape[0]
    layers = [_layer_params(l, norm_mix_g, norm_ffn_g, w_in, conv_a_w, conv_a_b, ln_a_g, ln_a_b, nat_rpb, pool_w,
                            pool_scale, dn_conv_w, dn_a_log, dn_dt_bias, dn_norm_g, w_branch, w_gate, b_gate, w_out,
                            w_ff1, w_ff2) for l in range(depth)]
    n_prompt = c_prompt.shape[0]
    mods = _ada(jnp.concatenate([c_prompt, c_sample], axis=0), w_ada, b_ada)
    fg = final_norm_g.reshape(1, -1).astype(F32)
    outs = []
    for x, lo, hi in ((x_prompt, 0, n_prompt), (x_sample, n_prompt, mods.shape[1])):
        bsz, seq_len, d = x.shape
        xt = x.reshape(bsz * seq_len, d)
        for l in range(depth):
            xt = _layer(xt, mods[l, lo:hi], layers[l], bsz, seq_len)
        outs.append(_final_norm(xt, fg).reshape(bsz, seq_len, d))
    return tuple(outs)
```

```python
import functools

import jax
import jax.numpy as jnp
import numpy as np
from jax import lax
from jax.experimental import pallas as pl
from jax.experimental.pallas import tpu as pltpu

F32 = jnp.float32
BF16 = jnp.bfloat16

D_MODEL = 2048
N_BRANCH = 4
BRANCH_W = D_MODEL // 4
D_FF = 4 * D_MODEL
EPS = 1e-6
NEG_INF = -1e30

CONV_A_K = 31
GRID_W = 64
NA_HEADS = 8
NA_HEAD_DIM = BRANCH_W // NA_HEADS
NA_KH = 8
NA_KW = 16
POOL_WINDOWS = (2, 4, 8, 16)
POOL_GW = BRANCH_W // len(POOL_WINDOWS)
DN_HEADS = 4
DN_DK = BRANCH_W // DN_HEADS
DN_CONV_K = 5
DN_CHUNK = 64

U_MAIN = 10 * BRANCH_W
U_SMALL = 4 * DN_HEADS
LANES = 128
SUBLANES = 8
CB_A_VAL, CB_A_GATE, CB_B_Q, CB_B_K, CB_B_V, CB_C, CB_D_Q, CB_D_K, CB_D_V, CB_D_GATE = range(10)

VMEM_LIMIT = 56 * 1024 * 1024
ROW_TILE = 1024
NORM_ROWS = 16
NORM_UNROLL = 8
HALO_TT = 256
HALO_A = 16
HALO_D = 8
NA_ROWS = 8
DN_SEQ_PER_STEP = 2


def _cparams(*sem):
    return pltpu.CompilerParams(dimension_semantics=sem, vmem_limit_bytes=VMEM_LIMIT)


def _sigmoid(x):
    return 1.0 / (1.0 + jnp.exp(-x))


def _silu(x):
    return x * _sigmoid(x)


def _dot(a, b):
    return jnp.dot(a, b, preferred_element_type=F32)


def _dot_nt(a, b):
    return lax.dot_general(a, b, (((1,), (1,)), ((), ())), preferred_element_type=F32)


def _ada_kernel(c_ref, w_ref, b_ref, o_ref):
    o_ref[...] = _dot(c_ref[...].astype(BF16), w_ref[...].astype(BF16)) + b_ref[...]


def _ada(c_all, w_ada, b_ada):
    nl, d, n6 = w_ada.shape
    nb = c_all.shape[0]
    tn = 1024
    out = pl.pallas_call(
        _ada_kernel,
        grid=(nl, n6 // tn),
        in_specs=[
            pl.BlockSpec((nb, d), lambda l, j: (0, 0)),
            pl.BlockSpec((None, d, tn), lambda l, j: (l, 0, j)),
            pl.BlockSpec((None, 1, tn), lambda l, j: (l, 0, j)),
        ],
        out_specs=pl.BlockSpec((None, nb, tn), lambda l, j: (l, 0, j)),
        out_shape=jax.ShapeDtypeStruct((nl, nb, n6), F32),
        compiler_params=_cparams("arbitrary", "arbitrary"),
        name="ada_mod",
    )(c_all, w_ada, b_ada.reshape(nl, 1, n6))
    return out.reshape(nl, nb, 6, d)


def _norm_mod(x_ref, g_ref, mod_ref, shift_idx, scale_idx, h_ref):
    gain = g_ref[...] * (1.0 + mod_ref[scale_idx:scale_idx + 1, :])
    shift = mod_ref[shift_idx:shift_idx + 1, :]

    def body(c, carry):
        r0 = pl.multiple_of(c * NORM_ROWS, NORM_ROWS)
        xf = x_ref[pl.ds(r0, NORM_ROWS), :]
        ms = jnp.mean(xf * xf, axis=-1, keepdims=True)
        h_ref[pl.ds(r0, NORM_ROWS), :] = (xf * lax.rsqrt(ms + EPS) * gain + shift).astype(h_ref.dtype)
        return carry

    lax.fori_loop(0, x_ref.shape[0] // NORM_ROWS, body, 0, unroll=NORM_UNROLL)


def _inproj_kernel(x_ref, g_ref, mod_ref, w_ref, ws_ref, u_ref, us_ref, h_ref):
    @pl.when(pl.program_id(1) == 0)
    def _():
        _norm_mod(x_ref, g_ref, mod_ref, 0, 1, h_ref)
        us_ref[...] = _dot(h_ref[...], ws_ref[...])

    u_ref[...] = _dot(h_ref[...], w_ref[...])


def _inproj(x, g, mods, w_main, w_small, seq_len):
    n, d = x.shape
    tm, tn = ROW_TILE, 1024
    tps = seq_len // tm
    return pl.pallas_call(
        _inproj_kernel,
        grid=(n // tm, U_MAIN // tn),
        in_specs=[
            pl.BlockSpec((tm, d), lambda i, j: (i, 0)),
            pl.BlockSpec((1, d), lambda i, j: (0, 0)),
            pl.BlockSpec((None, 6, d), lambda i, j: (i // tps, 0, 0)),
            pl.BlockSpec((d, tn), lambda i, j: (0, j)),
            pl.BlockSpec((d, LANES), lambda i, j: (0, 0)),
        ],
        out_specs=[
            pl.BlockSpec((tm, tn), lambda i, j: (i, j)),
            pl.BlockSpec((tm, LANES), lambda i, j: (i, 0)),
            pl.BlockSpec((tm, d), lambda i, j: (i, 0)),
        ],
        out_shape=[jax.ShapeDtypeStruct((n, U_MAIN), F32), jax.ShapeDtypeStruct((n, LANES), F32),
                   jax.ShapeDtypeStruct((n, d), BF16)],
        compiler_params=_cparams("arbitrary", "arbitrary"),
        name="in_proj",
    )(x, g, mods, w_main, w_small)


def _halo_specs(col_block, tt, halo, nt, bsz):
    per = tt // halo
    last = bsz * nt * per - 1

    def main(b, t):
        return (b * nt + t, col_block)

    def prev(b, t):
        return (jnp.maximum((b * nt + t) * per - 1, 0), col_block)

    def nxt(b, t):
        return (jnp.minimum((b * nt + t + 1) * per, last), col_block)

    return [pl.BlockSpec((tt, BRANCH_W), main), pl.BlockSpec((halo, BRANCH_W), prev),
            pl.BlockSpec((halo, BRANCH_W), nxt)]


def _fill_ext(ext_ref, main, prev, nxt, halo, tt):
    t = pl.program_id(1)
    ext_ref[halo:halo + tt, :] = main
    ext_ref[0:halo, :] = jnp.where(t > 0, prev, 0.0)
    ext_ref[halo + tt:2 * halo + tt, :] = jnp.where(t < pl.num_programs(1) - 1, nxt, 0.0)


def _conva_kernel(v_ref, pv_ref, nv_ref, g_ref, pg_ref, ng_ref, w_ref, b_ref, lg_ref, lb_ref, o_ref, a_ref, sh_ref):
    tt = o_ref.shape[0]
    _fill_ext(a_ref, v_ref[...] * _sigmoid(g_ref[...]), pv_ref[...] * _sigmoid(pg_ref[...]),
              nv_ref[...] * _sigmoid(ng_ref[...]), HALO_A, tt)
    rows = 32
    off = HALO_A - CONV_A_K // 2
    span = sh_ref.shape[1]
    for ph in range(1, SUBLANES):
        sh_ref[ph - 1] = a_ref[ph:ph + span, :]
    for c in range(tt // rows):
        parts = [jnp.zeros((SUBLANES, BRANCH_W), F32) + b_ref[...] for _ in range(rows // SUBLANES)]
        for k in range(CONV_A_K):
            ph, base = (k + off) % SUBLANES, c * rows + (k + off) // SUBLANES * SUBLANES
            wk = w_ref[k]
            for i in range(len(parts)):
                r0 = base + i * SUBLANES
                tap = a_ref[r0:r0 + SUBLANES, :] if ph == 0 else sh_ref[ph - 1, r0:r0 + SUBLANES, :]
                parts[i] = parts[i] + tap * wk
        acc = jnp.concatenate(parts, axis=0)
        mu = jnp.mean(acc, axis=-1, keepdims=True)
        dlt = acc - mu
        var = jnp.mean(dlt * dlt, axis=-1, keepdims=True)
        y = dlt * lax.rsqrt(var + EPS) * lg_ref[...] + lb_ref[...]
        o_ref[c * rows:(c + 1) * rows, :] = _silu(y).astype(o_ref.dtype)


def _mixer_a(u, conv_w, conv_b, ln_g, ln_b, bsz, seq_len):
    n = u.shape[0]
    tt = HALO_TT
    nt = seq_len // tt
    row = lambda b, t: (0, 0)
    return pl.pallas_call(
        _conva_kernel,
        grid=(bsz, nt),
        in_specs=_halo_specs(CB_A_VAL, tt, HALO_A, nt, bsz) + _halo_specs(CB_A_GATE, tt, HALO_A, nt, bsz) + [
            pl.BlockSpec((CONV_A_K, SUBLANES, BRANCH_W), lambda b, t: (0, 0, 0)),
            pl.BlockSpec((1, BRANCH_W), row), pl.BlockSpec((1, BRANCH_W), row), pl.BlockSpec((1, BRANCH_W), row),
        ],
        out_specs=pl.BlockSpec((tt, BRANCH_W), lambda b, t: (b * nt + t, 0)),
        out_shape=jax.ShapeDtypeStruct((n, BRANCH_W), BF16),
        scratch_shapes=[pltpu.VMEM((tt + 2 * HALO_A, BRANCH_W), F32),
                        pltpu.VMEM((SUBLANES - 1, tt + 2 * HALO_A - SUBLANES, BRANCH_W), F32)],
        compiler_params=_cparams("arbitrary", "arbitrary"),
        name="mixer_a_conv",
    )(u, u, u, u, u, u, conv_w, conv_b, ln_g, ln_b)


def _pool_kernel(x_ref, px_ref, nx_ref, pw_ref, ps_ref, o_ref, a_ref, *, seq_len):
    tt = o_ref.shape[0]
    _fill_ext(a_ref, x_ref[...], px_ref[...], nx_ref[...], HALO_A, tt)
    pos = pl.program_id(1) * tt + lax.broadcasted_iota(jnp.int32, (tt, 1), 0)
    for gi, win in enumerate(POOL_WINDOWS):
        half = win // 2
        lo, hi = gi * POOL_GW, (gi + 1) * POOL_GW
        acc = jnp.zeros((tt, POOL_GW), F32)
        for j in range(-half, half):
            acc = acc + a_ref[HALO_A + j:HALO_A + j + tt, lo:hi]
        cnt = (jnp.minimum(pos + half, seq_len) - jnp.maximum(pos - half, 0)).astype(F32)
        pooled = acc / cnt - x_ref[:, lo:hi]
        mixed = _dot(pooled.astype(BF16), pw_ref[gi])
        o_ref[:, lo:hi] = (mixed * ps_ref[:, lo:hi]).astype(o_ref.dtype)


def _mixer_c(u, pool_w, pool_scale, bsz, seq_len):
    n = u.shape[0]
    tt = HALO_TT
    nt = seq_len // tt
    return pl.pallas_call(
        functools.partial(_pool_kernel, seq_len=seq_len),
        grid=(bsz, nt),
        in_specs=_halo_specs(CB_C, tt, HALO_A, nt, bsz) + [
            pl.BlockSpec((len(POOL_WINDOWS), POOL_GW, POOL_GW), lambda b, t: (0, 0, 0)),
            pl.BlockSpec((1, BRANCH_W), lambda b, t: (0, 0)),
        ],
        out_specs=pl.BlockSpec((tt, BRANCH_W), lambda b, t: (b * nt + t, 0)),
        out_shape=jax.ShapeDtypeStruct((n, BRANCH_W), BF16),
        scratch_shapes=[pltpu.VMEM((tt + 2 * HALO_A, BRANCH_W), F32)],
        compiler_params=_cparams("arbitrary", "arbitrary"),
        name="mixer_c_pool",
    )(u, u, u, pool_w, pool_scale)


def _na_bias_table(rpb):
    c = np.arange(GRID_W)
    col_start = np.clip(c - NA_KW // 2, 0, GRID_W - NA_KW)
    col_ok = (c[None, :] >= col_start[:, None]) & (c[None, :] < col_start[:, None] + NA_KW)
    dc = np.clip(c[None, :] - c[:, None] + (NA_KW - 1), 0, 2 * NA_KW - 2)
    pick = (dc[None] == np.arange(2 * NA_KW - 1)[:, None, None]).astype(np.float32)
    cols = jnp.einsum("hrd,dqk->hrqk", rpb.astype(F32), pick, precision=lax.Precision.HIGHEST)
    cols = cols + np.where(col_ok, 0.0, NEG_INF).astype(np.float32)
    tab = jnp.stack([cols[:, a:a + NA_KH] for a in range(NA_KH)], axis=0)
    tab = jnp.transpose(tab, (0, 1, 3, 2, 4))
    return tab.reshape(NA_KH, NA_HEADS // 2, 2 * GRID_W, NA_KH * GRID_W)


def _na_kernel(q_ref, kp_ref, kc_ref, kn_ref, vp_ref, vc_ref, vn_ref, bt_ref, o_ref, k_ext, v_ext, *, n_rows):
    blk = NA_ROWS * GRID_W
    win = NA_KH * GRID_W
    j = pl.program_id(1)
    for i, (kr, vr) in enumerate(((kp_ref, vp_ref), (kc_ref, vc_ref), (kn_ref, vn_ref))):
        k_ext[i * blk:(i + 1) * blk, :] = kr[...].astype(BF16)
        v_ext[i * blk:(i + 1) * blk, :] = vr[...].astype(BF16)
    lane = lax.broadcasted_iota(jnp.int32, (1, LANES), 1)
    first_head = lane < NA_HEAD_DIM

    def row_body(ri, carry):
        r = j * NA_ROWS + ri
        rs = jnp.clip(r - NA_KH // 2, 0, n_rows - NA_KH)
        case = rs - r + (NA_KH - 1)
        start = pl.multiple_of((rs - j * NA_ROWS + NA_ROWS) * GRID_W, GRID_W)
        q0 = pl.multiple_of(ri * GRID_W, GRID_W)
        pairs = range(NA_HEADS // 2)
        lanes = [slice(hp * LANES, (hp + 1) * LANES) for hp in pairs]
        scores = []
        for hp in pairs:
            qp = q_ref[pl.ds(q0, GRID_W), lanes[hp]] * (NA_HEAD_DIM ** -0.5)
            qs = jnp.concatenate([jnp.where(first_head, qp, 0.0), jnp.where(first_head, 0.0, qp)], axis=0)
            scores.append(_dot_nt(qs.astype(BF16), k_ext[pl.ds(start, win), lanes[hp]]) + bt_ref[case, hp])
        probs, norms = [], []
        for s in scores:
            p = jnp.exp(s - jnp.max(s, axis=-1, keepdims=True))
            probs.append(p.astype(BF16))
            norms.append(jnp.sum(p, axis=-1, keepdims=True))
        for hp in pairs:
            pv = _dot(probs[hp], v_ext[pl.ds(start, win), lanes[hp]]) / norms[hp]
            o_ref[pl.ds(q0, GRID_W), lanes[hp]] = jnp.where(first_head, pv[:GRID_W], pv[GRID_W:]).astype(o_ref.dtype)
        return carry

    lax.fori_loop(0, NA_ROWS, row_body, 0, unroll=2)


def _mixer_b(u, bias_tab, bsz, seq_len):
    n = u.shape[0]
    blk = NA_ROWS * GRID_W
    n_rows = seq_len // GRID_W
    ng = seq_len // blk

    def cur(col):
        return lambda b, j: (b * ng + j, col)

    def prev(col):
        return lambda b, j: (b * ng + jnp.maximum(j - 1, 0), col)

    def nxt(col):
        return lambda b, j: (b * ng + jnp.minimum(j + 1, ng - 1), col)

    bs = lambda f: pl.BlockSpec((blk, BRANCH_W), f)
    return pl.pallas_call(
        functools.partial(_na_kernel, n_rows=n_rows),
        grid=(bsz, ng),
        in_specs=[bs(cur(CB_B_Q)), bs(prev(CB_B_K)), bs(cur(CB_B_K)), bs(nxt(CB_B_K)),
                  bs(prev(CB_B_V)), bs(cur(CB_B_V)), bs(nxt(CB_B_V)),
                  pl.BlockSpec(bias_tab.shape, lambda b, j: (0, 0, 0, 0))],
        out_specs=pl.BlockSpec((blk, BRANCH_W), lambda b, j: (b * ng + j, 0)),
        out_shape=jax.ShapeDtypeStruct((n, BRANCH_W), BF16),
        scratch_shapes=[pltpu.VMEM((3 * blk, BRANCH_W), BF16), pltpu.VMEM((3 * blk, BRANCH_W), BF16)],
        compiler_params=_cparams("arbitrary", "arbitrary"),
        name="mixer_b_na",
    )(u, u, u, u, u, u, u, bias_tab)


def _dprep_kernel(q_ref, pq_ref, nq_ref, k_ref, pk_ref, nk_ref, v_ref, pv_ref, nv_ref, us_ref,
                  w_ref, alog_ref, dt_ref, qo_ref, ko_ref, vo_ref, gb_ref, e_ref, sh_ref):
    tt = qo_ref.shape[0]
    off = HALO_D - DN_CONV_K // 2
    phases = sorted({(off + k) % SUBLANES for k in range(DN_CONV_K)} - {0})
    srcs = ((q_ref, pq_ref, nq_ref, qo_ref), (k_ref, pk_ref, nk_ref, ko_ref), (v_ref, pv_ref, nv_ref, vo_ref))
    for ci, (m_ref, p_ref, n_ref, out_ref) in enumerate(srcs):
        _fill_ext(e_ref, m_ref[...], p_ref[...], n_ref[...], HALO_D, tt)
        for idx, ph in enumerate(phases):
            sh_ref[idx] = e_ref[ph:ph + sh_ref.shape[1], :]
        acc = jnp.zeros((tt, BRANCH_W), F32)
        for k in range(DN_CONV_K):
            ph, base = (off + k) % SUBLANES, (off + k) // SUBLANES * SUBLANES
            tap = e_ref[base:base + tt, :] if ph == 0 else sh_ref[phases.index(ph), base:base + tt, :]
            acc = acc + tap * w_ref[k:k + 1, ci * BRANCH_W:(ci + 1) * BRANCH_W]
        y = _silu(acc)
        if ci == 2:
            out_ref[...] = y
        else:
            post = DN_DK ** -0.5 if ci == 0 else 1.0
            for h in range(DN_HEADS):
                yh = y[:, h * DN_DK:(h + 1) * DN_DK]
                inv = lax.rsqrt(jnp.sum(yh * yh, axis=-1, keepdims=True) + EPS)
                out_ref[:, h * DN_DK:(h + 1) * DN_DK] = yh * inv * post
    us = us_ref[...]
    lane = lax.broadcasted_iota(jnp.int32, us.shape, 1)
    z = us + dt_ref[...]
    softplus = jnp.maximum(z, 0.0) + jnp.log1p(jnp.exp(-jnp.abs(z)))
    gb_ref[...] = jnp.where(lane < 2 * DN_HEADS, _sigmoid(us),
                            jnp.where(lane < U_SMALL, -jnp.exp(alog_ref[...]) * softplus, 0.0))


def _mixer_d_prep(u, us, conv_w, alog_row, dt_row, bsz, seq_len):
    n = u.shape[0]
    tt = HALO_TT
    nt = seq_len // tt
    out_bs = pl.BlockSpec((tt, BRANCH_W), lambda b, t: (b * nt + t, 0))
    sm_bs = pl.BlockSpec((tt, LANES), lambda b, t: (b * nt + t, 0))
    row = lambda b, t: (0, 0)
    return pl.pallas_call(
        _dprep_kernel,
        grid=(bsz, nt),
        in_specs=(_halo_specs(CB_D_Q, tt, HALO_D, nt, bsz) + _halo_specs(CB_D_K, tt, HALO_D, nt, bsz)
                  + _halo_specs(CB_D_V, tt, HALO_D, nt, bsz)
                  + [sm_bs, pl.BlockSpec((DN_CONV_K, 3 * BRANCH_W), row),
                     pl.BlockSpec((1, LANES), row), pl.BlockSpec((1, LANES), row)]),
        out_specs=[out_bs, out_bs, out_bs, sm_bs],
        out_shape=[jax.ShapeDtypeStruct((n, BRANCH_W), F32)] * 3 + [jax.ShapeDtypeStruct((n, LANES), F32)],
        scratch_shapes=[pltpu.VMEM((tt + 2 * HALO_D, BRANCH_W), F32),
                        pltpu.VMEM((min(DN_CONV_K, SUBLANES) - 1, tt + HALO_D, BRANCH_W), F32)],
        compiler_params=_cparams("arbitrary", "arbitrary"),
        name="mixer_d_prep",
    )(u, u, u, u, u, u, u, u, u, us, conv_w, alog_row, dt_row)


def _dir_prep(gb_ref, rev):
    c = DN_CHUNK
    ri = lax.broadcasted_iota(jnp.int32, (c, LANES), 0)
    ci = lax.broadcasted_iota(jnp.int32, (c, LANES), 1) % c
    incl = (ri <= ci) if rev else (ri >= ci)
    strict = (ri < ci) if rev else (ri > ci)
    gb = gb_ref[...]
    g1 = gb.astype(BF16)
    r1 = gb - g1.astype(F32)
    g2 = r1.astype(BF16)
    g3 = (r1 - g2.astype(F32)).astype(BF16)
    tri = incl[:, :c].astype(BF16)
    cs = _dot(jnp.concatenate([tri, tri, tri], axis=1), jnp.concatenate([g1, g2, g3], axis=0))
    last = 0 if rev else c - 1
    cs_last = cs[last:last + 1, :]
    return dict(incl=incl, strict=strict, eye=(ri == ci).astype(F32), gb=gb, cs=cs,
                cs_t=jnp.concatenate([cs, cs], axis=0).T, e_in=jnp.exp(cs), e_tail=jnp.exp(cs_last - cs),
                g_tot=jnp.exp(cs_last))


def _delta_step(dirs, s_ref):
    c = DN_CHUNK
    left = lax.broadcasted_iota(jnp.int32, (1, LANES), 1) < c
    chains = []
    for q_ref, k_ref, v_ref, gb_ref, o_ref, rev in dirs:
        dp = _dir_prep(gb_ref, rev)
        for h in range(DN_HEADS):
            lb = int(rev) * DN_HEADS + h
            chains.append(dict(dp=dp, sl=slice(h * DN_DK, (h + 1) * DN_DK), lb=lb, lg=2 * DN_HEADS + lb,
                               q_ref=q_ref, k_ref=k_ref, v_ref=v_ref, o_ref=o_ref))
    for ch in chains:
        dp, sl, lg = ch["dp"], ch["sl"], ch["lg"]
        q, k, v = ch["q_ref"][:, sl], ch["k_ref"][:, sl], ch["v_ref"][:, sl]
        beta = dp["gb"][:, ch["lb"]:ch["lb"] + 1]
        e_col = dp["e_in"][:, lg:lg + 1]
        kb = k * beta
        k16 = k.astype(BF16)
        kk = _dot_nt(jnp.concatenate([kb, q], axis=0).astype(BF16), jnp.concatenate([k16, k16], axis=0))
        decay = jnp.exp(jnp.where(dp["incl"], dp["cs"][:, lg:lg + 1] - dp["cs_t"][lg:lg + 1, :], NEG_INF))
        a_mat = jnp.where(dp["strict"], kk[:c] * decay, 0.0)
        ch.update(w=jnp.where(left, dp["eye"], -a_mat), attn=(kk[c:] * decay)[:, :c],
                  rhs=jnp.concatenate([v * beta, kb * e_col], axis=1).astype(BF16),
                  q_dec=q * e_col, k_tail_t=(k * dp["e_tail"][:, lg:lg + 1]).T)
    n = 1
    while n < c:
        for ch in chains:
            w = ch["w"]
            hi = w.astype(BF16)
            lo = (w - hi.astype(F32)).astype(BF16)
            q_mix = jnp.where(left, pltpu.roll(w, c, 1).astype(BF16), lo)
            r = _dot(jnp.concatenate([q_mix, q_mix], axis=1), jnp.concatenate([hi, hi, lo, lo], axis=0))
            ch["w"] = jnp.where(left, w + r, r)
        n *= 2
    for ch in chains:
        ch["uw"] = _dot(ch["w"][:, :c].astype(BF16), ch["rhs"])
    for i, ch in enumerate(chains):
        state = s_ref[i]
        sw = _dot(jnp.concatenate([ch["uw"][:, DN_DK:], ch["q_dec"]], axis=0).astype(BF16), state.astype(BF16))
        ch.update(state=state, v_new=ch["uw"][:, :DN_DK] - sw[:c], qs=sw[c:])
    for i, ch in enumerate(chains):
        lg = ch["lg"]
        fin = _dot(jnp.concatenate([ch["attn"], ch["k_tail_t"]], axis=0).astype(BF16), ch["v_new"].astype(BF16))
        ch["o_ref"][:, ch["sl"]] = ch["qs"] + fin[:c]
        s_ref[i] = ch["state"] * ch["dp"]["g_tot"][:, lg:lg + 1] + fin[c:]


def _delta_kernel(qf, kf, vf, gf, qb, kb, vb, gbb, of_ref, ob_ref, s_ref):
    @pl.when(pl.program_id(1) == 0)
    def _():
        s_ref[...] = jnp.zeros(s_ref.shape, F32)

    dirs = []
    for s in range(qf.shape[0]):
        dirs += [(qf.at[s], kf.at[s], vf.at[s], gf.at[s], of_ref.at[s], False),
                 (qb.at[s], kb.at[s], vb.at[s], gbb.at[s], ob_ref.at[s], True)]
    _delta_step(dirs, s_ref)


def _mixer_d_scan(q, k, v, gb, bsz, seq_len):
    n = q.shape[0]
    nc = seq_len // DN_CHUNK
    n_seq = DN_SEQ_PER_STEP
    grouped = lambda a: a.reshape(bsz // n_seq, n_seq, seq_len, a.shape[-1])
    fwd = lambda b, m: (b, 0, m, 0)
    bwd = lambda b, m: (b, 0, nc - 1 - m, 0)
    wide = lambda f: pl.BlockSpec((None, n_seq, DN_CHUNK, BRANCH_W), f)
    small = lambda f: pl.BlockSpec((None, n_seq, DN_CHUNK, LANES), f)
    q, k, v, gb = grouped(q), grouped(k), grouped(v), grouped(gb)
    o_f, o_b = pl.pallas_call(
        _delta_kernel,
        grid=(bsz // n_seq, nc),
        in_specs=[wide(fwd), wide(fwd), wide(fwd), small(fwd), wide(bwd), wide(bwd), wide(bwd), small(bwd)],
        out_specs=[wide(fwd), wide(bwd)],
        out_shape=[jax.ShapeDtypeStruct(q.shape, F32)] * 2,
        scratch_shapes=[pltpu.VMEM((2 * DN_HEADS * n_seq, DN_DK, DN_DK), F32)],
        compiler_params=_cparams("arbitrary", "arbitrary"),
        name="mixer_d_scan",
    )(q, k, v, gb, q, k, v, gb)
    return o_f.reshape(n, BRANCH_W), o_b.reshape(n, BRANCH_W)


def _dout_kernel(of_ref, ob_ref, gate_ref, ng_ref, o_ref):
    o = of_ref[...] + ob_ref[...]
    gate = _silu(gate_ref[...])
    for h in range(DN_HEADS):
        sl = slice(h * DN_DK, (h + 1) * DN_DK)
        oh = o[:, sl]
        inv = lax.rsqrt(jnp.mean(oh * oh, axis=-1, keepdims=True) + EPS)
        o_ref[:, sl] = (oh * inv * ng_ref[...] * gate[:, sl]).astype(o_ref.dtype)


def _mixer_d_out(o_f, o_b, u, norm_g):
    n = o_f.shape[0]
    tt = 512
    bs = pl.BlockSpec((tt, BRANCH_W), lambda i: (i, 0))
    return pl.pallas_call(
        _dout_kernel,
        grid=(n // tt,),
        in_specs=[bs, bs, pl.BlockSpec((tt, BRANCH_W), lambda i: (i, CB_D_GATE)),
                  pl.BlockSpec((1, DN_DK), lambda i: (0, 0))],
        out_specs=bs,
        out_shape=jax.ShapeDtypeStruct((n, BRANCH_W), BF16),
        compiler_params=_cparams("arbitrary"),
        name="mixer_d_out",
    )(o_f, o_b, u, norm_g)


def _merge_kernel(h_ref, ya, yb, yc, yd, wg0, wg1, wg2, wg3, wb_ref, bg_ref, o_ref):
    acc = None
    for b, (y_ref, wg_ref) in enumerate(zip((ya, yb, yc, yd), (wg0, wg1, wg2, wg3))):
        gate = _sigmoid(_dot(h_ref[...], wg_ref[...]) + bg_ref[b:b + 1, :])
        term = gate * _dot(y_ref[...], wb_ref[b])
        acc = term if acc is None else acc + term
    o_ref[...] = acc.astype(o_ref.dtype)


def _merge(h, ys, w_gate, w_branch, b_gate):
    n, d = h.shape
    tm, tn = ROW_TILE, 512
    nj = d // tn
    ybs = pl.BlockSpec((tm, BRANCH_W), lambda i, j: (i, 0))

    def wg_spec(b):
        return pl.BlockSpec((d, tn), lambda i, j: (0, b * nj + j))

    return pl.pallas_call(
        _merge_kernel,
        grid=(n // tm, nj),
        in_specs=[
            pl.BlockSpec((tm, d), lambda i, j: (i, 0)),
            ybs, ybs, ybs, ybs,
            wg_spec(0), wg_spec(1), wg_spec(2), wg_spec(3),
            pl.BlockSpec((N_BRANCH, BRANCH_W, tn), lambda i, j: (0, 0, j)),
            pl.BlockSpec((N_BRANCH, tn), lambda i, j: (0, j)),
        ],
        out_specs=pl.BlockSpec((tm, tn), lambda i, j: (i, j)),
        out_shape=jax.ShapeDtypeStruct((n, d), BF16),
        compiler_params=_cparams("arbitrary", "arbitrary"),
        name="branch_merge",
    )(h, *ys, w_gate, w_gate, w_gate, w_gate, w_branch, b_gate)


def _outproj_kernel(m_ref, w_ref, x_ref, mod_ref, o_ref):
    o_ref[...] = x_ref[...] + mod_ref[2:3, :] * _dot(m_ref[...], w_ref[...])


def _outproj(merged, w_out, x, mods, seq_len):
    n, d = x.shape
    tm, tn = ROW_TILE, 1024
    tps = seq_len // tm
    return pl.pallas_call(
        _outproj_kernel,
        grid=(n // tm, d // tn),
        in_specs=[
            pl.BlockSpec((tm, d), lambda i, j: (i, 0)),
            pl.BlockSpec((d, tn), lambda i, j: (0, j)),
            pl.BlockSpec((tm, tn), lambda i, j: (i, j)),
            pl.BlockSpec((None, 6, tn), lambda i, j: (i // tps, 0, j)),
        ],
        out_specs=pl.BlockSpec((tm, tn), lambda i, j: (i, j)),
        out_shape=jax.ShapeDtypeStruct((n, d), F32),
        compiler_params=_cparams("arbitrary", "arbitrary"),
        name="out_proj",
    )(merged, w_out, x, mods)


def _ffn_kernel(x_ref, g_ref, mod_ref, w1_ref, w2_ref, fg_ref, o_ref, h_ref, *, final_norm):
    f = pl.program_id(1)

    @pl.when(f == 0)
    def _():
        _norm_mod(x_ref, g_ref, mod_ref, 3, 4, h_ref)
        o_ref[...] = x_ref[...]

    a = jnp.maximum(_dot(h_ref[...], w1_ref[...]), 0.0)
    o_ref[...] += mod_ref[5:6, :] * _dot((a * a).astype(BF16), w2_ref[...])

    if final_norm:
        @pl.when(f == pl.num_programs(1) - 1)
        def _():
            fg = fg_ref[...]

            def body(c, carry):
                r0 = pl.multiple_of(c * NORM_ROWS, NORM_ROWS)
                xf = o_ref[pl.ds(r0, NORM_ROWS), :]
                ms = jnp.mean(xf * xf, axis=-1, keepdims=True)
                o_ref[pl.ds(r0, NORM_ROWS), :] = xf * lax.rsqrt(ms + EPS) * fg
                return carry

            lax.fori_loop(0, o_ref.shape[0] // NORM_ROWS, body, 0, unroll=NORM_UNROLL)


def _ffn(x, g, mods, w1, w2, final_g, final_norm, seq_len):
    n, d = x.shape
    tm, tf = ROW_TILE, 512
    tps = seq_len // tm
    return pl.pallas_call(
        functools.partial(_ffn_kernel, final_norm=final_norm),
        grid=(n // tm, D_FF // tf),
        in_specs=[
            pl.BlockSpec((tm, d), lambda i, f: (i, 0)),
            pl.BlockSpec((1, d), lambda i, f: (0, 0)),
            pl.BlockSpec((None, 6, d), lambda i, f: (i // tps, 0, 0)),
            pl.BlockSpec((d, tf), lambda i, f: (0, f)),
            pl.BlockSpec((tf, d), lambda i, f: (f, 0)),
            pl.BlockSpec((1, d), lambda i, f: (0, 0)),
        ],
        out_specs=pl.BlockSpec((tm, d), lambda i, f: (i, 0)),
        out_shape=jax.ShapeDtypeStruct((n, d), F32),
        scratch_shapes=[pltpu.VMEM((tm, d), BF16)],
        compiler_params=_cparams("arbitrary", "arbitrary"),
        name="ffn",
    )(x, g, mods, w1, w2, final_g)


def _layer(x, mods, p, final_g, final_norm, bsz, seq_len):
    u, us, h = _inproj(x, p["norm_mix_g"], mods, p["w_in_main"], p["w_in_small"], seq_len)
    y_a = _mixer_a(u, p["conv_a_w"], p["conv_a_b"], p["ln_a_g"], p["ln_a_b"], bsz, seq_len)
    y_b = _mixer_b(u, p["na_bias"], bsz, seq_len)
    y_c = _mixer_c(u, p["pool_w"], p["pool_scale"], bsz, seq_len)
    q, k, v, gb = _mixer_d_prep(u, us, p["dn_conv_w"], p["dn_alog_row"], p["dn_dt_row"], bsz, seq_len)
    o_f, o_b = _mixer_d_scan(q, k, v, gb, bsz, seq_len)
    y_d = _mixer_d_out(o_f, o_b, u, p["dn_norm_g"])
    merged = _merge(h, (y_a, y_b, y_c, y_d), p["w_gate"], p["w_branch"], p["b_gate"])
    x = _outproj(merged, p["w_out"], x, mods, seq_len)
    return _ffn(x, p["norm_ffn_g"], mods, p["w_ff1"], p["w_ff2"], final_g, final_norm, seq_len)


def _layer_params(l, norm_mix_g, norm_ffn_g, w_in, conv_a_w, conv_a_b, ln_a_g, ln_a_b, nat_rpb, pool_w, pool_scale,
                  dn_conv_w, dn_a_log, dn_dt_bias, dn_norm_g, w_branch, w_gate, b_gate, w_out, w_ff1, w_ff2):
    d = D_MODEL
    row = lambda a: a.reshape(1, -1).astype(F32)
    small_pad = ((0, 0), (0, LANES - U_SMALL))
    lane_pad = lambda a: jnp.pad(a.reshape(1, -1).astype(F32), ((0, 0), (2 * DN_HEADS, LANES - U_SMALL)))
    return {
        "norm_mix_g": row(norm_mix_g[l]), "norm_ffn_g": row(norm_ffn_g[l]),
        "w_in_main": w_in[l][:, :U_MAIN].astype(BF16),
        "w_in_small": jnp.pad(w_in[l][:, U_MAIN:], small_pad).astype(BF16),
        "conv_a_w": jnp.broadcast_to(conv_a_w[l].astype(F32)[:, None, :], (CONV_A_K, SUBLANES, BRANCH_W)),
        "conv_a_b": row(conv_a_b[l]),
        "ln_a_g": row(ln_a_g[l]), "ln_a_b": row(ln_a_b[l]),
        "na_bias": _na_bias_table(nat_rpb[l]),
        "pool_w": pool_w[l].astype(BF16), "pool_scale": row(pool_scale[l]),
        "dn_conv_w": dn_conv_w[l].astype(F32),
        "dn_alog_row": lane_pad(dn_a_log[l]), "dn_dt_row": lane_pad(dn_dt_bias[l]),
        "dn_norm_g": row(dn_norm_g[l]),
        "w_branch": w_branch[l].astype(BF16),
        "w_gate": w_gate[l].reshape(d, N_BRANCH * d).astype(BF16),
        "b_gate": b_gate[l].astype(F32),
        "w_out": w_out[l].astype(BF16), "w_ff1": w_ff1[l].astype(BF16), "w_ff2": w_ff2[l].astype(BF16),
    }


def kernel(x_prompt, x_sample, c_prompt, c_sample, norm_mix_g, norm_ffn_g, w_ada, b_ada, w_in, conv_a_w, conv_a_b,
           ln_a_g, ln_a_b, nat_rpb, pool_w, pool_scale, dn_conv_w, dn_a_log, dn_dt_bias, dn_norm_g, w_branch, w_gate,
           b_gate, w_out, w_ff1, w_ff2, final_norm_g):
    depth = w_in.shape[0]
    layers = [_layer_params(l, norm_mix_g, norm_ffn_g, w_in, conv_a_w, conv_a_b, ln_a_g, ln_a_b, nat_rpb, pool_w,
                            pool_scale, dn_conv_w, dn_a_log, dn_dt_bias, dn_norm_g, w_branch, w_gate, b_gate, w_out,
                            w_ff1, w_ff2) for l in range(depth)]
    n_prompt = c_prompt.shape[0]
    mods = _ada(jnp.concatenate([c_prompt, c_sample], axis=0), w_ada, b_ada)
    fg = final_norm_g.reshape(1, -1).astype(F32)
    outs = []
    for x, lo, hi in ((x_prompt, 0, n_prompt), (x_sample, n_prompt, mods.shape[1])):
        bsz, seq_len, d = x.shape
        xt = x.reshape(bsz * seq_len, d)
        for l in range(depth):
            xt = _layer(xt, mods[l, lo:hi], layers[l], fg, l == depth - 1, bsz, seq_len)
        outs.append(xt.reshape(bsz, seq_len, d))
    return tuple(outs)
```

```python
import functools

import jax
import jax.numpy as jnp
import numpy as np
from jax import lax
from jax.experimental import pallas as pl
from jax.experimental.pallas import tpu as pltpu

F32 = jnp.float32
BF16 = jnp.bfloat16

D_MODEL = 2048
N_BRANCH = 4
BRANCH_W = D_MODEL // 4
D_FF = 4 * D_MODEL
EPS = 1e-6
NEG_INF = -1e30

CONV_A_K = 31
GRID_W = 64
NA_HEADS = 8
NA_HEAD_DIM = BRANCH_W // NA_HEADS
NA_KH = 8
NA_KW = 16
POOL_WINDOWS = (2, 4, 8, 16)
POOL_GW = BRANCH_W // len(POOL_WINDOWS)
DN_HEADS = 4
DN_DK = BRANCH_W // DN_HEADS
DN_CONV_K = 5
DN_CHUNK = 64

U_MAIN = 10 * BRANCH_W
U_SMALL = 4 * DN_HEADS
LANES = 128
SUBLANES = 8
CB_A_VAL, CB_A_GATE, CB_B_Q, CB_B_K, CB_B_V, CB_C, CB_D_Q, CB_D_K, CB_D_V, CB_D_GATE = range(10)

VMEM_LIMIT = 56 * 1024 * 1024
ROW_TILE = 1024
NORM_ROWS = 16
NORM_UNROLL = 8
HALO_TT = 256
HALO_A = 16
HALO_D = 8
NA_ROWS = 8
DN_SEQ_PER_STEP = 4


def _cparams(*sem):
    return pltpu.CompilerParams(dimension_semantics=sem, vmem_limit_bytes=VMEM_LIMIT)


def _sigmoid(x):
    return 1.0 / (1.0 + jnp.exp(-x))


def _silu(x):
    return x * _sigmoid(x)


def _dot(a, b):
    return jnp.dot(a, b, preferred_element_type=F32)


def _dot_nt(a, b):
    return lax.dot_general(a, b, (((1,), (1,)), ((), ())), preferred_element_type=F32)


def _ada_kernel(c_ref, w_ref, b_ref, o_ref):
    o_ref[...] = _dot(c_ref[...].astype(BF16), w_ref[...].astype(BF16)) + b_ref[...]


def _ada(c_all, w_ada, b_ada):
    nl, d, n6 = w_ada.shape
    nb = c_all.shape[0]
    tn = 1024
    out = pl.pallas_call(
        _ada_kernel,
        grid=(nl, n6 // tn),
        in_specs=[
            pl.BlockSpec((nb, d), lambda l, j: (0, 0)),
            pl.BlockSpec((None, d, tn), lambda l, j: (l, 0, j)),
            pl.BlockSpec((None, 1, tn), lambda l, j: (l, 0, j)),
        ],
        out_specs=pl.BlockSpec((None, nb, tn), lambda l, j: (l, 0, j)),
        out_shape=jax.ShapeDtypeStruct((nl, nb, n6), F32),
        compiler_params=_cparams("arbitrary", "arbitrary"),
        name="ada_mod",
    )(c_all, w_ada, b_ada.reshape(nl, 1, n6))
    return out.reshape(nl, nb, 6, d)


def _norm_mod(x_ref, g_ref, mod_ref, shift_idx, scale_idx, h_ref):
    gain = g_ref[...] * (1.0 + mod_ref[scale_idx:scale_idx + 1, :])
    shift = mod_ref[shift_idx:shift_idx + 1, :]

    def body(c, carry):
        r0 = pl.multiple_of(c * NORM_ROWS, NORM_ROWS)
        xf = x_ref[pl.ds(r0, NORM_ROWS), :]
        ms = jnp.mean(xf * xf, axis=-1, keepdims=True)
        h_ref[pl.ds(r0, NORM_ROWS), :] = (xf * lax.rsqrt(ms + EPS) * gain + shift).astype(h_ref.dtype)
        return carry

    lax.fori_loop(0, x_ref.shape[0] // NORM_ROWS, body, 0, unroll=NORM_UNROLL)


def _inproj_kernel(x_ref, g_ref, mod_ref, w_ref, u_ref, h_ref):
    @pl.when(pl.program_id(1) == 0)
    def _():
        _norm_mod(x_ref, g_ref, mod_ref, 0, 1, h_ref)

    u_ref[...] = _dot(h_ref[...], w_ref[...])


def _inproj(x, g, mods, w_main, seq_len):
    n, d = x.shape
    tm, tn = ROW_TILE, 1024
    tps = seq_len // tm
    return pl.pallas_call(
        _inproj_kernel,
        grid=(n // tm, U_MAIN // tn),
        in_specs=[
            pl.BlockSpec((tm, d), lambda i, j: (i, 0)),
            pl.BlockSpec((1, d), lambda i, j: (0, 0)),
            pl.BlockSpec((None, 6, d), lambda i, j: (i // tps, 0, 0)),
            pl.BlockSpec((d, tn), lambda i, j: (0, j)),
        ],
        out_specs=[
            pl.BlockSpec((tm, tn), lambda i, j: (i, j)),
            pl.BlockSpec((tm, d), lambda i, j: (i, 0)),
        ],
        out_shape=[jax.ShapeDtypeStruct((n, U_MAIN), F32), jax.ShapeDtypeStruct((n, d), BF16)],
        compiler_params=_cparams("arbitrary", "arbitrary"),
        name="in_proj",
    )(x, g, mods, w_main)


def _halo_specs(col_block, tt, halo, nt, bsz):
    per = tt // halo
    last = bsz * nt * per - 1

    def main(b, t):
        return (b * nt + t, col_block)

    def prev(b, t):
        return (jnp.maximum((b * nt + t) * per - 1, 0), col_block)

    def nxt(b, t):
        return (jnp.minimum((b * nt + t + 1) * per, last), col_block)

    return [pl.BlockSpec((tt, BRANCH_W), main), pl.BlockSpec((halo, BRANCH_W), prev),
            pl.BlockSpec((halo, BRANCH_W), nxt)]


def _fill_ext(ext_ref, main, prev, nxt, halo, tt):
    t = pl.program_id(1)
    ext_ref[halo:halo + tt, :] = main
    ext_ref[0:halo, :] = jnp.where(t > 0, prev, 0.0)
    ext_ref[halo + tt:2 * halo + tt, :] = jnp.where(t < pl.num_programs(1) - 1, nxt, 0.0)


def _conva_kernel(v_ref, pv_ref, nv_ref, g_ref, pg_ref, ng_ref, w_ref, b_ref, lg_ref, lb_ref, o_ref, a_ref, sh_ref):
    tt = o_ref.shape[0]
    _fill_ext(a_ref, v_ref[...] * _sigmoid(g_ref[...]), pv_ref[...] * _sigmoid(pg_ref[...]),
              nv_ref[...] * _sigmoid(ng_ref[...]), HALO_A, tt)
    rows = 32
    off = HALO_A - CONV_A_K // 2
    span = sh_ref.shape[1]
    for ph in range(1, SUBLANES):
        sh_ref[ph - 1] = a_ref[ph:ph + span, :]
    for c in range(tt // rows):
        parts = [jnp.zeros((SUBLANES, BRANCH_W), F32) + b_ref[...] for _ in range(rows // SUBLANES)]
        for k in range(CONV_A_K):
            ph, base = (k + off) % SUBLANES, c * rows + (k + off) // SUBLANES * SUBLANES
            wk = w_ref[k]
            for i in range(len(parts)):
                r0 = base + i * SUBLANES
                tap = a_ref[r0:r0 + SUBLANES, :] if ph == 0 else sh_ref[ph - 1, r0:r0 + SUBLANES, :]
                parts[i] = parts[i] + tap * wk
        acc = jnp.concatenate(parts, axis=0)
        mu = jnp.mean(acc, axis=-1, keepdims=True)
        dlt = acc - mu
        var = jnp.mean(dlt * dlt, axis=-1, keepdims=True)
        y = dlt * lax.rsqrt(var + EPS) * lg_ref[...] + lb_ref[...]
        o_ref[c * rows:(c + 1) * rows, :] = _silu(y).astype(o_ref.dtype)


def _mixer_a(u, conv_w, conv_b, ln_g, ln_b, bsz, seq_len):
    n = u.shape[0]
    tt = HALO_TT
    nt = seq_len // tt
    row = lambda b, t: (0, 0)
    return pl.pallas_call(
        _conva_kernel,
        grid=(bsz, nt),
        in_specs=_halo_specs(CB_A_VAL, tt, HALO_A, nt, bsz) + _halo_specs(CB_A_GATE, tt, HALO_A, nt, bsz) + [
            pl.BlockSpec((CONV_A_K, SUBLANES, BRANCH_W), lambda b, t: (0, 0, 0)),
            pl.BlockSpec((1, BRANCH_W), row), pl.BlockSpec((1, BRANCH_W), row), pl.BlockSpec((1, BRANCH_W), row),
        ],
        out_specs=pl.BlockSpec((tt, BRANCH_W), lambda b, t: (b * nt + t, 0)),
        out_shape=jax.ShapeDtypeStruct((n, BRANCH_W), BF16),
        scratch_shapes=[pltpu.VMEM((tt + 2 * HALO_A, BRANCH_W), F32),
                        pltpu.VMEM((SUBLANES - 1, tt + 2 * HALO_A - SUBLANES, BRANCH_W), F32)],
        compiler_params=_cparams("arbitrary", "arbitrary"),
        name="mixer_a_conv",
    )(u, u, u, u, u, u, conv_w, conv_b, ln_g, ln_b)


def _pool_kernel(x_ref, px_ref, nx_ref, pw_ref, ps_ref, o_ref, a_ref, *, seq_len):
    tt = o_ref.shape[0]
    _fill_ext(a_ref, x_ref[...], px_ref[...], nx_ref[...], HALO_A, tt)
    pos = pl.program_id(1) * tt + lax.broadcasted_iota(jnp.int32, (tt, 1), 0)
    for gi, win in enumerate(POOL_WINDOWS):
        half = win // 2
        lo, hi = gi * POOL_GW, (gi + 1) * POOL_GW
        acc = jnp.zeros((tt, POOL_GW), F32)
        for j in range(-half, half):
            acc = acc + a_ref[HALO_A + j:HALO_A + j + tt, lo:hi]
        cnt = (jnp.minimum(pos + half, seq_len) - jnp.maximum(pos - half, 0)).astype(F32)
        pooled = acc / cnt - x_ref[:, lo:hi]
        mixed = _dot(pooled.astype(BF16), pw_ref[gi])
        o_ref[:, lo:hi] = (mixed * ps_ref[:, lo:hi]).astype(o_ref.dtype)


def _mixer_c(u, pool_w, pool_scale, bsz, seq_len):
    n = u.shape[0]
    tt = HALO_TT
    nt = seq_len // tt
    return pl.pallas_call(
        functools.partial(_pool_kernel, seq_len=seq_len),
        grid=(bsz, nt),
        in_specs=_halo_specs(CB_C, tt, HALO_A, nt, bsz) + [
            pl.BlockSpec((len(POOL_WINDOWS), POOL_GW, POOL_GW), lambda b, t: (0, 0, 0)),
            pl.BlockSpec((1, BRANCH_W), lambda b, t: (0, 0)),
        ],
        out_specs=pl.BlockSpec((tt, BRANCH_W), lambda b, t: (b * nt + t, 0)),
        out_shape=jax.ShapeDtypeStruct((n, BRANCH_W), BF16),
        scratch_shapes=[pltpu.VMEM((tt + 2 * HALO_A, BRANCH_W), F32)],
        compiler_params=_cparams("arbitrary", "arbitrary"),
        name="mixer_c_pool",
    )(u, u, u, pool_w, pool_scale)


def _na_bias_table(rpb):
    c = np.arange(GRID_W)
    col_start = np.clip(c - NA_KW // 2, 0, GRID_W - NA_KW)
    col_ok = (c[None, :] >= col_start[:, None]) & (c[None, :] < col_start[:, None] + NA_KW)
    dc = np.clip(c[None, :] - c[:, None] + (NA_KW - 1), 0, 2 * NA_KW - 2)
    pick = (dc[None] == np.arange(2 * NA_KW - 1)[:, None, None]).astype(np.float32)
    cols = jnp.einsum("hrd,dqk->hrqk", rpb.astype(F32), pick, precision=lax.Precision.HIGHEST)
    cols = cols + np.where(col_ok, 0.0, NEG_INF).astype(np.float32)
    tab = jnp.stack([cols[:, a:a + NA_KH] for a in range(NA_KH)], axis=0)
    tab = jnp.transpose(tab, (0, 1, 3, 2, 4))
    return tab.reshape(NA_KH, NA_HEADS // 2, 2 * GRID_W, NA_KH * GRID_W)


def _na_kernel(q_ref, kp_ref, kc_ref, kn_ref, vp_ref, vc_ref, vn_ref, bt_ref, o_ref, k_ext, v_ext, *, n_rows):
    blk = NA_ROWS * GRID_W
    win = NA_KH * GRID_W
    j = pl.program_id(1)
    for i, (kr, vr) in enumerate(((kp_ref, vp_ref), (kc_ref, vc_ref), (kn_ref, vn_ref))):
        k_ext[i * blk:(i + 1) * blk, :] = kr[...].astype(BF16)
        v_ext[i * blk:(i + 1) * blk, :] = vr[...].astype(BF16)
    lane = lax.broadcasted_iota(jnp.int32, (1, LANES), 1)
    first_head = lane < NA_HEAD_DIM

    def row_body(ri, carry):
        r = j * NA_ROWS + ri
        rs = jnp.clip(r - NA_KH // 2, 0, n_rows - NA_KH)
        case = rs - r + (NA_KH - 1)
        start = pl.multiple_of((rs - j * NA_ROWS + NA_ROWS) * GRID_W, GRID_W)
        q0 = pl.multiple_of(ri * GRID_W, GRID_W)
        pairs = range(NA_HEADS // 2)
        lanes = [slice(hp * LANES, (hp + 1) * LANES) for hp in pairs]
        scores = []
        for hp in pairs:
            qp = q_ref[pl.ds(q0, GRID_W), lanes[hp]] * (NA_HEAD_DIM ** -0.5)
            qs = jnp.concatenate([jnp.where(first_head, qp, 0.0), jnp.where(first_head, 0.0, qp)], axis=0)
            scores.append(_dot_nt(qs.astype(BF16), k_ext[pl.ds(start, win), lanes[hp]]) + bt_ref[case, hp])
        probs, norms = [], []
        for s in scores:
            p = jnp.exp(s - jnp.max(s, axis=-1, keepdims=True))
            probs.append(p.astype(BF16))
            norms.append(jnp.sum(p, axis=-1, keepdims=True))
        for hp in pairs:
            pv = _dot(probs[hp], v_ext[pl.ds(start, win), lanes[hp]]) / norms[hp]
            o_ref[pl.ds(q0, GRID_W), lanes[hp]] = jnp.where(first_head, pv[:GRID_W], pv[GRID_W:]).astype(o_ref.dtype)
        return carry

    lax.fori_loop(0, NA_ROWS, row_body, 0, unroll=2)


def _mixer_b(u, bias_tab, bsz, seq_len):
    n = u.shape[0]
    blk = NA_ROWS * GRID_W
    n_rows = seq_len // GRID_W
    ng = seq_len // blk

    def cur(col):
        return lambda b, j: (b * ng + j, col)

    def prev(col):
        return lambda b, j: (b * ng + jnp.maximum(j - 1, 0), col)

    def nxt(col):
        return lambda b, j: (b * ng + jnp.minimum(j + 1, ng - 1), col)

    bs = lambda f: pl.BlockSpec((blk, BRANCH_W), f)
    return pl.pallas_call(
        functools.partial(_na_kernel, n_rows=n_rows),
        grid=(bsz, ng),
        in_specs=[bs(cur(CB_B_Q)), bs(prev(CB_B_K)), bs(cur(CB_B_K)), bs(nxt(CB_B_K)),
                  bs(prev(CB_B_V)), bs(cur(CB_B_V)), bs(nxt(CB_B_V)),
                  pl.BlockSpec(bias_tab.shape, lambda b, j: (0, 0, 0, 0))],
        out_specs=pl.BlockSpec((blk, BRANCH_W), lambda b, j: (b * ng + j, 0)),
        out_shape=jax.ShapeDtypeStruct((n, BRANCH_W), BF16),
        scratch_shapes=[pltpu.VMEM((3 * blk, BRANCH_W), BF16), pltpu.VMEM((3 * blk, BRANCH_W), BF16)],
        compiler_params=_cparams("arbitrary", "arbitrary"),
        name="mixer_b_na",
    )(u, u, u, u, u, u, u, bias_tab)


def _dprep_kernel(q_ref, pq_ref, nq_ref, k_ref, pk_ref, nk_ref, v_ref, pv_ref, nv_ref, h_ref, ws_ref,
                  w_ref, alog_ref, dt_ref, qo_ref, ko_ref, vo_ref, gb_ref, e_ref, sh_ref):
    tt = qo_ref.shape[0]
    off = HALO_D - DN_CONV_K // 2
    phases = sorted({(off + k) % SUBLANES for k in range(DN_CONV_K)} - {0})
    srcs = ((q_ref, pq_ref, nq_ref, qo_ref), (k_ref, pk_ref, nk_ref, ko_ref), (v_ref, pv_ref, nv_ref, vo_ref))
    for ci, (m_ref, p_ref, n_ref, out_ref) in enumerate(srcs):
        _fill_ext(e_ref, m_ref[...], p_ref[...], n_ref[...], HALO_D, tt)
        for idx, ph in enumerate(phases):
            sh_ref[idx] = e_ref[ph:ph + sh_ref.shape[1], :]
        acc = jnp.zeros((tt, BRANCH_W), F32)
        for k in range(DN_CONV_K):
            ph, base = (off + k) % SUBLANES, (off + k) // SUBLANES * SUBLANES
            tap = e_ref[base:base + tt, :] if ph == 0 else sh_ref[phases.index(ph), base:base + tt, :]
            acc = acc + tap * w_ref[k:k + 1, ci * BRANCH_W:(ci + 1) * BRANCH_W]
        y = _silu(acc)
        if ci == 2:
            out_ref[...] = y
        else:
            post = DN_DK ** -0.5 if ci == 0 else 1.0
            for h in range(DN_HEADS):
                yh = y[:, h * DN_DK:(h + 1) * DN_DK]
                inv = lax.rsqrt(jnp.sum(yh * yh, axis=-1, keepdims=True) + EPS)
                out_ref[:, h * DN_DK:(h + 1) * DN_DK] = yh * inv * post
    us = _dot(h_ref[...], ws_ref[...])
    lane = lax.broadcasted_iota(jnp.int32, us.shape, 1)
    z = us + dt_ref[...]
    softplus = jnp.maximum(z, 0.0) + jnp.log1p(jnp.exp(-jnp.abs(z)))
    gb_ref[...] = jnp.where(lane < 2 * DN_HEADS, _sigmoid(us),
                            jnp.where(lane < U_SMALL, -jnp.exp(alog_ref[...]) * softplus, 0.0))


def _mixer_d_prep(u, h, w_small, conv_w, alog_row, dt_row, bsz, seq_len):
    n, d = h.shape
    tt = HALO_TT
    nt = seq_len // tt
    out_bs = pl.BlockSpec((tt, BRANCH_W), lambda b, t: (b * nt + t, 0))
    sm_bs = pl.BlockSpec((tt, LANES), lambda b, t: (b * nt + t, 0))
    row = lambda b, t: (0, 0)
    return pl.pallas_call(
        _dprep_kernel,
        grid=(bsz, nt),
        in_specs=(_halo_specs(CB_D_Q, tt, HALO_D, nt, bsz) + _halo_specs(CB_D_K, tt, HALO_D, nt, bsz)
                  + _halo_specs(CB_D_V, tt, HALO_D, nt, bsz)
                  + [pl.BlockSpec((tt, d), lambda b, t: (b * nt + t, 0)), pl.BlockSpec((d, LANES), row),
                     pl.BlockSpec((DN_CONV_K, 3 * BRANCH_W), row),
                     pl.BlockSpec((1, LANES), row), pl.BlockSpec((1, LANES), row)]),
        out_specs=[out_bs, out_bs, out_bs, sm_bs],
        out_shape=[jax.ShapeDtypeStruct((n, BRANCH_W), F32)] * 3 + [jax.ShapeDtypeStruct((n, LANES), F32)],
        scratch_shapes=[pltpu.VMEM((tt + 2 * HALO_D, BRANCH_W), F32),
                        pltpu.VMEM((min(DN_CONV_K, SUBLANES) - 1, tt + HALO_D, BRANCH_W), F32)],
        compiler_params=_cparams("arbitrary", "arbitrary"),
        name="mixer_d_prep",
    )(u, u, u, u, u, u, u, u, u, h, w_small, conv_w, alog_row, dt_row)


def _dir_prep(gb_ref, rev):
    c = DN_CHUNK
    ri = lax.broadcasted_iota(jnp.int32, (c, LANES), 0)
    ci = lax.broadcasted_iota(jnp.int32, (c, LANES), 1) % c
    incl = (ri <= ci) if rev else (ri >= ci)
    strict = (ri < ci) if rev else (ri > ci)
    gb = gb_ref[...]
    g1 = gb.astype(BF16)
    r1 = gb - g1.astype(F32)
    g2 = r1.astype(BF16)
    g3 = (r1 - g2.astype(F32)).astype(BF16)
    tri = incl[:, :c].astype(BF16)
    cs = _dot(jnp.concatenate([tri, tri, tri], axis=1), jnp.concatenate([g1, g2, g3], axis=0))
    last = 0 if rev else c - 1
    cs_last = cs[last:last + 1, :]
    return dict(incl=incl, strict=strict, eye=(ri == ci).astype(F32), gb=gb, cs=cs,
                cs_t=jnp.concatenate([cs, cs], axis=0).T, e_in=jnp.exp(cs), e_tail=jnp.exp(cs_last - cs),
                g_tot=jnp.exp(cs_last))


def _delta_step(dirs, s_ref):
    c = DN_CHUNK
    left = lax.broadcasted_iota(jnp.int32, (1, LANES), 1) < c
    chains = []
    for q_ref, k_ref, v_ref, gb_ref, o_ref, rev in dirs:
        dp = _dir_prep(gb_ref, rev)
        for h in range(DN_HEADS):
            lb = int(rev) * DN_HEADS + h
            chains.append(dict(dp=dp, sl=slice(h * DN_DK, (h + 1) * DN_DK), lb=lb, lg=2 * DN_HEADS + lb,
                               q_ref=q_ref, k_ref=k_ref, v_ref=v_ref, o_ref=o_ref))
    for ch in chains:
        dp, sl, lg = ch["dp"], ch["sl"], ch["lg"]
        q, k, v = ch["q_ref"][:, sl], ch["k_ref"][:, sl], ch["v_ref"][:, sl]
        beta = dp["gb"][:, ch["lb"]:ch["lb"] + 1]
        e_col = dp["e_in"][:, lg:lg + 1]
        kb = k * beta
        k16 = k.astype(BF16)
        kk = _dot_nt(jnp.concatenate([kb, q], axis=0).astype(BF16), jnp.concatenate([k16, k16], axis=0))
        decay = jnp.exp(jnp.where(dp["incl"], dp["cs"][:, lg:lg + 1] - dp["cs_t"][lg:lg + 1, :], NEG_INF))
        a_mat = jnp.where(dp["strict"], kk[:c] * decay, 0.0)
        ch.update(w=jnp.where(left, dp["eye"], -a_mat), attn=(kk[c:] * decay)[:, :c],
                  rhs=jnp.concatenate([v * beta, kb * e_col], axis=1).astype(BF16),
                  q_dec=q * e_col, k_tail_t=(k * dp["e_tail"][:, lg:lg + 1]).T)
    n = 1
    while n < c:
        for ch in chains:
            w = ch["w"]
            hi = w.astype(BF16)
            lo = (w - hi.astype(F32)).astype(BF16)
            q_mix = jnp.where(left, pltpu.roll(w, c, 1).astype(BF16), lo)
            r = _dot(jnp.concatenate([q_mix, q_mix], axis=1), jnp.concatenate([hi, hi, lo, lo], axis=0))
            ch["w"] = jnp.where(left, w + r, r)
        n *= 2
    for ch in chains:
        ch["uw"] = _dot(ch["w"][:, :c].astype(BF16), ch["rhs"])
    for i, ch in enumerate(chains):
        state = s_ref[i]
        sw = _dot(jnp.concatenate([ch["uw"][:, DN_DK:], ch["q_dec"]], axis=0).astype(BF16), state.astype(BF16))
        ch.update(state=state, v_new=ch["uw"][:, :DN_DK] - sw[:c], qs=sw[c:])
    for i, ch in enumerate(chains):
        lg = ch["lg"]
        fin = _dot(jnp.concatenate([ch["attn"], ch["k_tail_t"]], axis=0).astype(BF16), ch["v_new"].astype(BF16))
        ch["o_ref"][:, ch["sl"]] = ch["qs"] + fin[:c]
        s_ref[i] = ch["state"] * ch["dp"]["g_tot"][:, lg:lg + 1] + fin[c:]


def _delta_kernel(qf, kf, vf, gf, qb, kb, vb, gbb, of_ref, ob_ref, s_ref):
    @pl.when(pl.program_id(1) == 0)
    def _():
        s_ref[...] = jnp.zeros(s_ref.shape, F32)

    dirs = []
    for s in range(qf.shape[0]):
        dirs += [(qf.at[s], kf.at[s], vf.at[s], gf.at[s], of_ref.at[s], False),
                 (qb.at[s], kb.at[s], vb.at[s], gbb.at[s], ob_ref.at[s], True)]
    _delta_step(dirs, s_ref)


def _mixer_d_scan(q, k, v, gb, bsz, seq_len):
    n = q.shape[0]
    nc = seq_len // DN_CHUNK
    n_seq = DN_SEQ_PER_STEP if bsz % DN_SEQ_PER_STEP == 0 else 2
    assert bsz % n_seq == 0, (bsz, n_seq)
    grouped = lambda a: a.reshape(bsz // n_seq, n_seq, seq_len, a.shape[-1])
    fwd = lambda b, m: (b, 0, m, 0)
    bwd = lambda b, m: (b, 0, nc - 1 - m, 0)
    wide = lambda f: pl.BlockSpec((None, n_seq, DN_CHUNK, BRANCH_W), f)
    small = lambda f: pl.BlockSpec((None, n_seq, DN_CHUNK, LANES), f)
    q, k, v, gb = grouped(q), grouped(k), grouped(v), grouped(gb)
    o_f, o_b = pl.pallas_call(
        _delta_kernel,
        grid=(bsz // n_seq, nc),
        in_specs=[wide(fwd), wide(fwd), wide(fwd), small(fwd), wide(bwd), wide(bwd), wide(bwd), small(bwd)],
        out_specs=[wide(fwd), wide(bwd)],
        out_shape=[jax.ShapeDtypeStruct(q.shape, F32)] * 2,
        scratch_shapes=[pltpu.VMEM((2 * DN_HEADS * n_seq, DN_DK, DN_DK), F32)],
        compiler_params=_cparams("arbitrary", "arbitrary"),
        name="mixer_d_scan",
    )(q, k, v, gb, q, k, v, gb)
    return o_f.reshape(n, BRANCH_W), o_b.reshape(n, BRANCH_W)


def _dout_kernel(of_ref, ob_ref, gate_ref, ng_ref, o_ref):
    o = of_ref[...] + ob_ref[...]
    gate = _silu(gate_ref[...])
    for h in range(DN_HEADS):
        sl = slice(h * DN_DK, (h + 1) * DN_DK)
        oh = o[:, sl]
        inv = lax.rsqrt(jnp.mean(oh * oh, axis=-1, keepdims=True) + EPS)
        o_ref[:, sl] = (oh * inv * ng_ref[...] * gate[:, sl]).astype(o_ref.dtype)


def _mixer_d_out(o_f, o_b, u, norm_g):
    n = o_f.shape[0]
    tt = 512
    bs = pl.BlockSpec((tt, BRANCH_W), lambda i: (i, 0))
    return pl.pallas_call(
        _dout_kernel,
        grid=(n // tt,),
        in_specs=[bs, bs, pl.BlockSpec((tt, BRANCH_W), lambda i: (i, CB_D_GATE)),
                  pl.BlockSpec((1, DN_DK), lambda i: (0, 0))],
        out_specs=bs,
        out_shape=jax.ShapeDtypeStruct((n, BRANCH_W), BF16),
        compiler_params=_cparams("arbitrary"),
        name="mixer_d_out",
    )(o_f, o_b, u, norm_g)


def _merge_kernel(h_ref, ya, yb, yc, yd, wg0, wg1, wg2, wg3, wb_ref, bg_ref, o_ref):
    acc = None
    for b, (y_ref, wg_ref) in enumerate(zip((ya, yb, yc, yd), (wg0, wg1, wg2, wg3))):
        gate = _sigmoid(_dot(h_ref[...], wg_ref[...]) + bg_ref[b:b + 1, :])
        term = gate * _dot(y_ref[...], wb_ref[b])
        acc = term if acc is None else acc + term
    o_ref[...] = acc.astype(o_ref.dtype)


def _merge(h, ys, w_gate, w_branch, b_gate):
    n, d = h.shape
    tm, tn = ROW_TILE, 512
    nj = d // tn
    ybs = pl.BlockSpec((tm, BRANCH_W), lambda i, j: (i, 0))

    def wg_spec(b):
        return pl.BlockSpec((d, tn), lambda i, j: (0, b * nj + j))

    return pl.pallas_call(
        _merge_kernel,
        grid=(n // tm, nj),
        in_specs=[
            pl.BlockSpec((tm, d), lambda i, j: (i, 0)),
            ybs, ybs, ybs, ybs,
            wg_spec(0), wg_spec(1), wg_spec(2), wg_spec(3),
            pl.BlockSpec((N_BRANCH, BRANCH_W, tn), lambda i, j: (0, 0, j)),
            pl.BlockSpec((N_BRANCH, tn), lambda i, j: (0, j)),
        ],
        out_specs=pl.BlockSpec((tm, tn), lambda i, j: (i, j)),
        out_shape=jax.ShapeDtypeStruct((n, d), BF16),
        compiler_params=_cparams("arbitrary", "arbitrary"),
        name="branch_merge",
    )(h, *ys, w_gate, w_gate, w_gate, w_gate, w_branch, b_gate)


def _outproj_kernel(m_ref, w_ref, x_ref, mod_ref, o_ref):
    o_ref[...] = x_ref[...] + mod_ref[2:3, :] * _dot(m_ref[...], w_ref[...])


def _outproj(merged, w_out, x, mods, seq_len):
    n, d = x.shape
    tm, tn = ROW_TILE, 1024
    tps = seq_len // tm
    return pl.pallas_call(
        _outproj_kernel,
        grid=(n // tm, d // tn),
        in_specs=[
            pl.BlockSpec((tm, d), lambda i, j: (i, 0)),
            pl.BlockSpec((d, tn), lambda i, j: (0, j)),
            pl.BlockSpec((tm, tn), lambda i, j: (i, j)),
            pl.BlockSpec((None, 6, tn), lambda i, j: (i // tps, 0, j)),
        ],
        out_specs=pl.BlockSpec((tm, tn), lambda i, j: (i, j)),
        out_shape=jax.ShapeDtypeStruct((n, d), F32),
        compiler_params=_cparams("arbitrary", "arbitrary"),
        name="out_proj",
    )(merged, w_out, x, mods)


def _ffn_kernel(x_ref, g_ref, mod_ref, w1_ref, w2_ref, fg_ref, o_ref, h_ref, *, final_norm):
    f = pl.program_id(1)

    @pl.when(f == 0)
    def _():
        _norm_mod(x_ref, g_ref, mod_ref, 3, 4, h_ref)
        o_ref[...] = x_ref[...]

    a = jnp.maximum(_dot(h_ref[...], w1_ref[...]), 0.0)
    o_ref[...] += mod_ref[5:6, :] * _dot((a * a).astype(BF16), w2_ref[...])

    if final_norm:
        @pl.when(f == pl.num_programs(1) - 1)
        def _():
            fg = fg_ref[...]

            group = NORM_ROWS * NORM_UNROLL

            def body(c, carry):
                r0 = pl.multiple_of(c * group, group)
                rows = [pl.ds(r0 + k * NORM_ROWS, NORM_ROWS) for k in range(NORM_UNROLL)]
                xs = [o_ref[r, :] for r in rows]
                ys = [xf * lax.rsqrt(jnp.mean(xf * xf, axis=-1, keepdims=True) + EPS) * fg for xf in xs]
                for r, y in zip(rows, ys):
                    o_ref[r, :] = y
                return carry

            lax.fori_loop(0, o_ref.shape[0] // group, body, 0)


def _ffn(x, g, mods, w1, w2, final_g, final_norm, seq_len):
    n, d = x.shape
    tm, tf = ROW_TILE, 512
    tps = seq_len // tm
    return pl.pallas_call(
        functools.partial(_ffn_kernel, final_norm=final_norm),
        grid=(n // tm, D_FF // tf),
        in_specs=[
            pl.BlockSpec((tm, d), lambda i, f: (i, 0)),
            pl.BlockSpec((1, d), lambda i, f: (0, 0)),
            pl.BlockSpec((None, 6, d), lambda i, f: (i // tps, 0, 0)),
            pl.BlockSpec((d, tf), lambda i, f: (0, f)),
            pl.BlockSpec((tf, d), lambda i, f: (f, 0)),
            pl.BlockSpec((1, d), lambda i, f: (0, 0)),
        ],
        out_specs=pl.BlockSpec((tm, d), lambda i, f: (i, 0)),
        out_shape=jax.ShapeDtypeStruct((n, d), F32),
        scratch_shapes=[pltpu.VMEM((tm, d), BF16)],
        compiler_params=_cparams("arbitrary", "arbitrary"),
        name="ffn",
    )(x, g, mods, w1, w2, final_g)


def _layer(x, mods, p, final_g, final_norm, bsz, seq_len):
    u, h = _inproj(x, p["norm_mix_g"], mods, p["w_in_main"], seq_len)
    y_a = _mixer_a(u, p["conv_a_w"], p["conv_a_b"], p["ln_a_g"], p["ln_a_b"], bsz, seq_len)
    y_b = _mixer_b(u, p["na_bias"], bsz, seq_len)
    y_c = _mixer_c(u, p["pool_w"], p["pool_scale"], bsz, seq_len)
    q, k, v, gb = _mixer_d_prep(u, h, p["w_in_small"], p["dn_conv_w"], p["dn_alog_row"], p["dn_dt_row"], bsz,
                                seq_len)
    o_f, o_b = _mixer_d_scan(q, k, v, gb, bsz, seq_len)
    y_d = _mixer_d_out(o_f, o_b, u, p["dn_norm_g"])
    merged = _merge(h, (y_a, y_b, y_c, y_d), p["w_gate"], p["w_branch"], p["b_gate"])
    x = _outproj(merged, p["w_out"], x, mods, seq_len)
    return _ffn(x, p["norm_ffn_g"], mods, p["w_ff1"], p["w_ff2"], final_g, final_norm, seq_len)


def _layer_params(l, norm_mix_g, norm_ffn_g, w_in, conv_a_w, conv_a_b, ln_a_g, ln_a_b, nat_rpb, pool_w, pool_scale,
                  dn_conv_w, dn_a_log, dn_dt_bias, dn_norm_g, w_branch, w_gate, b_gate, w_out, w_ff1, w_ff2):
    d = D_MODEL
    row = lambda a: a.reshape(1, -1).astype(F32)
    small_pad = ((0, 0), (0, LANES - U_SMALL))
    lane_pad = lambda a: jnp.pad(a.reshape(1, -1).astype(F32), ((0, 0), (2 * DN_HEADS, LANES - U_SMALL)))
    return {
        "norm_mix_g": row(norm_mix_g[l]), "norm_ffn_g": row(norm_ffn_g[l]),
        "w_in_main": w_in[l][:, :U_MAIN].astype(BF16),
        "w_in_small": jnp.pad(w_in[l][:, U_MAIN:], small_pad).astype(BF16),
        "conv_a_w": jnp.broadcast_to(conv_a_w[l].astype(F32)[:, None, :], (CONV_A_K, SUBLANES, BRANCH_W)),
        "conv_a_b": row(conv_a_b[l]),
        "ln_a_g": row(ln_a_g[l]), "ln_a_b": row(ln_a_b[l]),
        "na_bias": _na_bias_table(nat_rpb[l]),
        "pool_w": pool_w[l].astype(BF16), "pool_scale": row(pool_scale[l]),
        "dn_conv_w": dn_conv_w[l].astype(F32),
        "dn_alog_row": lane_pad(dn_a_log[l]), "dn_dt_row": lane_pad(dn_dt_bias[l]),
        "dn_norm_g": row(dn_norm_g[l]),
        "w_branch": w_branch[l].astype(BF16),
        "w_gate": w_gate[l].reshape(d, N_BRANCH * d).astype(BF16),
        "b_gate": b_gate[l].astype(F32),
        "w_out": w_out[l].astype(BF16), "w_ff1": w_ff1[l].astype(BF16), "w_ff2": w_ff2[l].astype(BF16),
    }


def kernel(x_prompt, x_sample, c_prompt, c_sample, norm_mix_g, norm_ffn_g, w_ada, b_ada, w_in, conv_a_w, conv_a_b,
           ln_a_g, ln_a_b, nat_rpb, pool_w, pool_scale, dn_conv_w, dn_a_log, dn_dt_bias, dn_norm_g, w_branch, w_gate,
           b_gate, w_out, w_ff1, w_ff2, final_norm_g):
    depth = w_in.shape[0]
    layers = [_layer_params(l, norm_mix_g, norm_ffn_g, w_in, conv_a_w, conv_a_b, ln_a_g, ln_a_b, nat_rpb, pool_w,
                            pool_scale, dn_conv_w, dn_a_log, dn_dt_bias, dn_norm_g, w_branch, w_gate, b_gate, w_out,
                            w_ff1, w_ff2) for l in range(depth)]
    n_prompt = c_prompt.shape[0]
    mods = _ada(jnp.concatenate([c_prompt, c_sample], axis=0), w_ada, b_ada)
    fg = final_norm_g.reshape(1, -1).astype(F32)
    outs = []
    for x, lo, hi in ((x_prompt, 0, n_prompt), (x_sample, n_prompt, mods.shape[1])):
        bsz, seq_len, d = x.shape
        xt = x.reshape(bsz * seq_len, d)
        for l in range(depth):
            xt = _layer(xt, mods[l, lo:hi], layers[l], fg, l == depth - 1, bsz, seq_len)
        outs.append(xt.reshape(bsz, seq_len, d))
    return tuple(outs)
```

```python
import functools

import jax
import jax.numpy as jnp
import numpy as np
from jax import lax
from jax.experimental import pallas as pl
from jax.experimental.pallas import tpu as pltpu

F32 = jnp.float32
BF16 = jnp.bfloat16

D_MODEL = 2048
N_BRANCH = 4
BRANCH_W = D_MODEL // 4
D_FF = 4 * D_MODEL
EPS = 1e-6
NEG_INF = -1e30

CONV_A_K = 31
GRID_W = 64
NA_HEADS = 8
NA_HEAD_DIM = BRANCH_W // NA_HEADS
NA_KH = 8
NA_KW = 16
POOL_WINDOWS = (2, 4, 8, 16)
POOL_GW = BRANCH_W // len(POOL_WINDOWS)
DN_HEADS = 4
DN_DK = BRANCH_W // DN_HEADS
DN_CONV_K = 5
DN_CHUNK = 64

U_MAIN = 10 * BRANCH_W
U_SMALL = 4 * DN_HEADS
LANES = 128
SUBLANES = 8
CB_A_VAL, CB_A_GATE, CB_B_Q, CB_B_K, CB_B_V, CB_C, CB_D_Q, CB_D_K, CB_D_V, CB_D_GATE = range(10)

VMEM_LIMIT = 56 * 1024 * 1024
ROW_TILE = 1024
NORM_ROWS = 16
NORM_UNROLL = 8
HALO_TT = 256
HALO_A = 16
HALO_D = 8
NA_ROWS = 8
DN_SEQ_PER_STEP = 4


def _cparams(*sem):
    return pltpu.CompilerParams(dimension_semantics=sem, vmem_limit_bytes=VMEM_LIMIT)


def _sigmoid(x):
    return 1.0 / (1.0 + jnp.exp(-x))


def _silu(x):
    return x * _sigmoid(x)


def _dot(a, b):
    return jnp.dot(a, b, preferred_element_type=F32)


def _dot_nt(a, b):
    return lax.dot_general(a, b, (((1,), (1,)), ((), ())), preferred_element_type=F32)


def _ada_kernel(c_ref, w_ref, b_ref, o_ref):
    o_ref[...] = _dot(c_ref[...].astype(BF16), w_ref[...].astype(BF16)) + b_ref[...]


def _ada(c_all, w_ada, b_ada):
    nl, d, n6 = w_ada.shape
    nb = c_all.shape[0]
    tn = 1024
    out = pl.pallas_call(
        _ada_kernel,
        grid=(nl, n6 // tn),
        in_specs=[
            pl.BlockSpec((nb, d), lambda l, j: (0, 0)),
            pl.BlockSpec((None, d, tn), lambda l, j: (l, 0, j)),
            pl.BlockSpec((None, 1, tn), lambda l, j: (l, 0, j)),
        ],
        out_specs=pl.BlockSpec((None, nb, tn), lambda l, j: (l, 0, j)),
        out_shape=jax.ShapeDtypeStruct((nl, nb, n6), F32),
        compiler_params=_cparams("arbitrary", "arbitrary"),
        name="ada_mod",
    )(c_all, w_ada, b_ada.reshape(nl, 1, n6))
    return out.reshape(nl, nb, 6, d)


def _norm_mod(x_ref, g_ref, mod_ref, shift_idx, scale_idx, h_ref):
    gain = g_ref[...] * (1.0 + mod_ref[scale_idx:scale_idx + 1, :])
    shift = mod_ref[shift_idx:shift_idx + 1, :]

    def body(c, carry):
        r0 = pl.multiple_of(c * NORM_ROWS, NORM_ROWS)
        xf = x_ref[pl.ds(r0, NORM_ROWS), :]
        ms = jnp.mean(xf * xf, axis=-1, keepdims=True)
        h_ref[pl.ds(r0, NORM_ROWS), :] = (xf * lax.rsqrt(ms + EPS) * gain + shift).astype(h_ref.dtype)
        return carry

    lax.fori_loop(0, x_ref.shape[0] // NORM_ROWS, body, 0, unroll=NORM_UNROLL)


def _inproj_kernel(x_ref, g_ref, mod_ref, w_ref, u_ref, h_ref):
    @pl.when(pl.program_id(1) == 0)
    def _():
        _norm_mod(x_ref, g_ref, mod_ref, 0, 1, h_ref)

    u_ref[...] = _dot(h_ref[...], w_ref[...])


def _inproj(x, g, mods, w_main, seq_len):
    n, d = x.shape
    tm, tn = ROW_TILE, 1024
    tps = seq_len // tm
    return pl.pallas_call(
        _inproj_kernel,
        grid=(n // tm, U_MAIN // tn),
        in_specs=[
            pl.BlockSpec((tm, d), lambda i, j: (i, 0)),
            pl.BlockSpec((1, d), lambda i, j: (0, 0)),
            pl.BlockSpec((None, 6, d), lambda i, j: (i // tps, 0, 0)),
            pl.BlockSpec((d, tn), lambda i, j: (0, j)),
        ],
        out_specs=[
            pl.BlockSpec((tm, tn), lambda i, j: (i, j)),
            pl.BlockSpec((tm, d), lambda i, j: (i, 0)),
        ],
        out_shape=[jax.ShapeDtypeStruct((n, U_MAIN), F32), jax.ShapeDtypeStruct((n, d), BF16)],
        compiler_params=_cparams("arbitrary", "arbitrary"),
        name="in_proj",
    )(x, g, mods, w_main)


def _halo_specs(col_block, tt, halo, nt, bsz):
    per = tt // halo
    last = bsz * nt * per - 1

    def main(b, t):
        return (b * nt + t, col_block)

    def prev(b, t):
        return (jnp.maximum((b * nt + t) * per - 1, 0), col_block)

    def nxt(b, t):
        return (jnp.minimum((b * nt + t + 1) * per, last), col_block)

    return [pl.BlockSpec((tt, BRANCH_W), main), pl.BlockSpec((halo, BRANCH_W), prev),
            pl.BlockSpec((halo, BRANCH_W), nxt)]


def _fill_ext(ext_ref, main, prev, nxt, halo, tt):
    t = pl.program_id(1)
    ext_ref[halo:halo + tt, :] = main
    ext_ref[0:halo, :] = jnp.where(t > 0, prev, 0.0)
    ext_ref[halo + tt:2 * halo + tt, :] = jnp.where(t < pl.num_programs(1) - 1, nxt, 0.0)


def _conva_kernel(v_ref, pv_ref, nv_ref, g_ref, pg_ref, ng_ref, w_ref, b_ref, lg_ref, lb_ref, o_ref, a_ref, sh_ref):
    tt = o_ref.shape[0]
    _fill_ext(a_ref, v_ref[...] * _sigmoid(g_ref[...]), pv_ref[...] * _sigmoid(pg_ref[...]),
              nv_ref[...] * _sigmoid(ng_ref[...]), HALO_A, tt)
    rows = 32
    off = HALO_A - CONV_A_K // 2
    span = sh_ref.shape[1]
    for ph in range(1, SUBLANES):
        sh_ref[ph - 1] = a_ref[ph:ph + span, :]
    for c in range(tt // rows):
        parts = [jnp.zeros((SUBLANES, BRANCH_W), F32) + b_ref[...] for _ in range(rows // SUBLANES)]
        for k in range(CONV_A_K):
            ph, base = (k + off) % SUBLANES, c * rows + (k + off) // SUBLANES * SUBLANES
            wk = w_ref[k]
            for i in range(len(parts)):
                r0 = base + i * SUBLANES
                tap = a_ref[r0:r0 + SUBLANES, :] if ph == 0 else sh_ref[ph - 1, r0:r0 + SUBLANES, :]
                parts[i] = parts[i] + tap * wk
        acc = jnp.concatenate(parts, axis=0)
        mu = jnp.mean(acc, axis=-1, keepdims=True)
        dlt = acc - mu
        var = jnp.mean(dlt * dlt, axis=-1, keepdims=True)
        y = dlt * lax.rsqrt(var + EPS) * lg_ref[...] + lb_ref[...]
        o_ref[c * rows:(c + 1) * rows, :] = _silu(y).astype(o_ref.dtype)


def _mixer_a(u, conv_w, conv_b, ln_g, ln_b, bsz, seq_len):
    n = u.shape[0]
    tt = HALO_TT
    nt = seq_len // tt
    row = lambda b, t: (0, 0)
    return pl.pallas_call(
        _conva_kernel,
        grid=(bsz, nt),
        in_specs=_halo_specs(CB_A_VAL, tt, HALO_A, nt, bsz) + _halo_specs(CB_A_GATE, tt, HALO_A, nt, bsz) + [
            pl.BlockSpec((CONV_A_K, SUBLANES, BRANCH_W), lambda b, t: (0, 0, 0)),
            pl.BlockSpec((1, BRANCH_W), row), pl.BlockSpec((1, BRANCH_W), row), pl.BlockSpec((1, BRANCH_W), row),
        ],
        out_specs=pl.BlockSpec((tt, BRANCH_W), lambda b, t: (b * nt + t, 0)),
        out_shape=jax.ShapeDtypeStruct((n, BRANCH_W), BF16),
        scratch_shapes=[pltpu.VMEM((tt + 2 * HALO_A, BRANCH_W), F32),
                        pltpu.VMEM((SUBLANES - 1, tt + 2 * HALO_A - SUBLANES, BRANCH_W), F32)],
        compiler_params=_cparams("arbitrary", "arbitrary"),
        name="mixer_a_conv",
    )(u, u, u, u, u, u, conv_w, conv_b, ln_g, ln_b)


def _pool_kernel(x_ref, px_ref, nx_ref, pw_ref, ps_ref, o_ref, a_ref, *, seq_len):
    tt = o_ref.shape[0]
    _fill_ext(a_ref, x_ref[...], px_ref[...], nx_ref[...], HALO_A, tt)
    pos = pl.program_id(1) * tt + lax.broadcasted_iota(jnp.int32, (tt, 1), 0)
    for gi, win in enumerate(POOL_WINDOWS):
        half = win // 2
        lo, hi = gi * POOL_GW, (gi + 1) * POOL_GW
        acc = jnp.zeros((tt, POOL_GW), F32)
        for j in range(-half, half):
            acc = acc + a_ref[HALO_A + j:HALO_A + j + tt, lo:hi]
        cnt = (jnp.minimum(pos + half, seq_len) - jnp.maximum(pos - half, 0)).astype(F32)
        pooled = acc / cnt - x_ref[:, lo:hi]
        mixed = _dot(pooled.astype(BF16), pw_ref[gi])
        o_ref[:, lo:hi] = (mixed * ps_ref[:, lo:hi]).astype(o_ref.dtype)


def _mixer_c(u, pool_w, pool_scale, bsz, seq_len):
    n = u.shape[0]
    tt = HALO_TT
    nt = seq_len // tt
    return pl.pallas_call(
        functools.partial(_pool_kernel, seq_len=seq_len),
        grid=(bsz, nt),
        in_specs=_halo_specs(CB_C, tt, HALO_A, nt, bsz) + [
            pl.BlockSpec((len(POOL_WINDOWS), POOL_GW, POOL_GW), lambda b, t: (0, 0, 0)),
            pl.BlockSpec((1, BRANCH_W), lambda b, t: (0, 0)),
        ],
        out_specs=pl.BlockSpec((tt, BRANCH_W), lambda b, t: (b * nt + t, 0)),
        out_shape=jax.ShapeDtypeStruct((n, BRANCH_W), BF16),
        scratch_shapes=[pltpu.VMEM((tt + 2 * HALO_A, BRANCH_W), F32)],
        compiler_params=_cparams("arbitrary", "arbitrary"),
        name="mixer_c_pool",
    )(u, u, u, pool_w, pool_scale)


def _na_bias_table(rpb):
    c = np.arange(GRID_W)
    col_start = np.clip(c - NA_KW // 2, 0, GRID_W - NA_KW)
    col_ok = (c[None, :] >= col_start[:, None]) & (c[None, :] < col_start[:, None] + NA_KW)
    dc = np.clip(c[None, :] - c[:, None] + (NA_KW - 1), 0, 2 * NA_KW - 2)
    pick = (dc[None] == np.arange(2 * NA_KW - 1)[:, None, None]).astype(np.float32)
    cols = jnp.einsum("hrd,dqk->hrqk", rpb.astype(F32), pick, precision=lax.Precision.HIGHEST)
    cols = cols + np.where(col_ok, 0.0, NEG_INF).astype(np.float32)
    tab = jnp.stack([cols[:, a:a + NA_KH] for a in range(NA_KH)], axis=0)
    tab = jnp.transpose(tab, (0, 1, 3, 2, 4))
    return tab.reshape(NA_KH, NA_HEADS // 2, 2 * GRID_W, NA_KH * GRID_W)


def _na_kernel(q_ref, kp_ref, kc_ref, kn_ref, vp_ref, vc_ref, vn_ref, bt_ref, o_ref, k_ext, v_ext, *, n_rows):
    blk = NA_ROWS * GRID_W
    win = NA_KH * GRID_W
    j = pl.program_id(1)
    for i, (kr, vr) in enumerate(((kp_ref, vp_ref), (kc_ref, vc_ref), (kn_ref, vn_ref))):
        k_ext[i * blk:(i + 1) * blk, :] = kr[...].astype(BF16)
        v_ext[i * blk:(i + 1) * blk, :] = vr[...].astype(BF16)
    lane = lax.broadcasted_iota(jnp.int32, (1, LANES), 1)
    first_head = lane < NA_HEAD_DIM

    def row_body(ri, carry):
        r = j * NA_ROWS + ri
        rs = jnp.clip(r - NA_KH // 2, 0, n_rows - NA_KH)
        case = rs - r + (NA_KH - 1)
        start = pl.multiple_of((rs - j * NA_ROWS + NA_ROWS) * GRID_W, GRID_W)
        q0 = pl.multiple_of(ri * GRID_W, GRID_W)
        pairs = range(NA_HEADS // 2)
        lanes = [slice(hp * LANES, (hp + 1) * LANES) for hp in pairs]
        scores = []
        for hp in pairs:
            qp = q_ref[pl.ds(q0, GRID_W), lanes[hp]] * (NA_HEAD_DIM ** -0.5)
            qs = jnp.concatenate([jnp.where(first_head, qp, 0.0), jnp.where(first_head, 0.0, qp)], axis=0)
            scores.append(_dot_nt(qs.astype(BF16), k_ext[pl.ds(start, win), lanes[hp]]) + bt_ref[case, hp])
        probs, norms = [], []
        for s in scores:
            p = jnp.exp(s - jnp.max(s, axis=-1, keepdims=True))
            probs.append(p.astype(BF16))
            norms.append(jnp.sum(p, axis=-1, keepdims=True))
        for hp in pairs:
            pv = _dot(probs[hp], v_ext[pl.ds(start, win), lanes[hp]]) / norms[hp]
            o_ref[pl.ds(q0, GRID_W), lanes[hp]] = jnp.where(first_head, pv[:GRID_W], pv[GRID_W:]).astype(o_ref.dtype)
        return carry

    lax.fori_loop(0, NA_ROWS, row_body, 0, unroll=4)


def _mixer_b(u, bias_tab, bsz, seq_len):
    n = u.shape[0]
    blk = NA_ROWS * GRID_W
    n_rows = seq_len // GRID_W
    ng = seq_len // blk

    def cur(col):
        return lambda b, j: (b * ng + j, col)

    def prev(col):
        return lambda b, j: (b * ng + jnp.maximum(j - 1, 0), col)

    def nxt(col):
        return lambda b, j: (b * ng + jnp.minimum(j + 1, ng - 1), col)

    bs = lambda f: pl.BlockSpec((blk, BRANCH_W), f)
    return pl.pallas_call(
        functools.partial(_na_kernel, n_rows=n_rows),
        grid=(bsz, ng),
        in_specs=[bs(cur(CB_B_Q)), bs(prev(CB_B_K)), bs(cur(CB_B_K)), bs(nxt(CB_B_K)),
                  bs(prev(CB_B_V)), bs(cur(CB_B_V)), bs(nxt(CB_B_V)),
                  pl.BlockSpec(bias_tab.shape, lambda b, j: (0, 0, 0, 0))],
        out_specs=pl.BlockSpec((blk, BRANCH_W), lambda b, j: (b * ng + j, 0)),
        out_shape=jax.ShapeDtypeStruct((n, BRANCH_W), BF16),
        scratch_shapes=[pltpu.VMEM((3 * blk, BRANCH_W), BF16), pltpu.VMEM((3 * blk, BRANCH_W), BF16)],
        compiler_params=_cparams("arbitrary", "arbitrary"),
        name="mixer_b_na",
    )(u, u, u, u, u, u, u, bias_tab)


def _dprep_kernel(q_ref, pq_ref, nq_ref, k_ref, pk_ref, nk_ref, v_ref, pv_ref, nv_ref, h_ref, ws_ref,
                  w_ref, alog_ref, dt_ref, qo_ref, ko_ref, vo_ref, gb_ref, e_ref, sh_ref):
    tt = qo_ref.shape[0]
    off = HALO_D - DN_CONV_K // 2
    phases = sorted({(off + k) % SUBLANES for k in range(DN_CONV_K)} - {0})
    srcs = ((q_ref, pq_ref, nq_ref, qo_ref), (k_ref, pk_ref, nk_ref, ko_ref), (v_ref, pv_ref, nv_ref, vo_ref))
    for ci, (m_ref, p_ref, n_ref, out_ref) in enumerate(srcs):
        _fill_ext(e_ref, m_ref[...], p_ref[...], n_ref[...], HALO_D, tt)
        for idx, ph in enumerate(phases):
            sh_ref[idx] = e_ref[ph:ph + sh_ref.shape[1], :]
        acc = jnp.zeros((tt, BRANCH_W), F32)
        for k in range(DN_CONV_K):
            ph, base = (off + k) % SUBLANES, (off + k) // SUBLANES * SUBLANES
            tap = e_ref[base:base + tt, :] if ph == 0 else sh_ref[phases.index(ph), base:base + tt, :]
            acc = acc + tap * w_ref[k:k + 1, ci * BRANCH_W:(ci + 1) * BRANCH_W]
        y = _silu(acc)
        if ci == 2:
            out_ref[...] = y
        else:
            post = DN_DK ** -0.5 if ci == 0 else 1.0
            for h in range(DN_HEADS):
                yh = y[:, h * DN_DK:(h + 1) * DN_DK]
                inv = lax.rsqrt(jnp.sum(yh * yh, axis=-1, keepdims=True) + EPS)
                out_ref[:, h * DN_DK:(h + 1) * DN_DK] = yh * inv * post
    us = _dot(h_ref[...], ws_ref[...])
    lane = lax.broadcasted_iota(jnp.int32, us.shape, 1)
    z = us + dt_ref[...]
    softplus = jnp.maximum(z, 0.0) + jnp.log1p(jnp.exp(-jnp.abs(z)))
    gb_ref[...] = jnp.where(lane < 2 * DN_HEADS, _sigmoid(us),
                            jnp.where(lane < U_SMALL, -jnp.exp(alog_ref[...]) * softplus, 0.0))


def _mixer_d_prep(u, h, w_small, conv_w, alog_row, dt_row, bsz, seq_len):
    n, d = h.shape
    tt = HALO_TT
    nt = seq_len // tt
    out_bs = pl.BlockSpec((tt, BRANCH_W), lambda b, t: (b * nt + t, 0))
    sm_bs = pl.BlockSpec((tt, LANES), lambda b, t: (b * nt + t, 0))
    row = lambda b, t: (0, 0)
    return pl.pallas_call(
        _dprep_kernel,
        grid=(bsz, nt),
        in_specs=(_halo_specs(CB_D_Q, tt, HALO_D, nt, bsz) + _halo_specs(CB_D_K, tt, HALO_D, nt, bsz)
                  + _halo_specs(CB_D_V, tt, HALO_D, nt, bsz)
                  + [pl.BlockSpec((tt, d), lambda b, t: (b * nt + t, 0)), pl.BlockSpec((d, LANES), row),
                     pl.BlockSpec((DN_CONV_K, 3 * BRANCH_W), row),
                     pl.BlockSpec((1, LANES), row), pl.BlockSpec((1, LANES), row)]),
        out_specs=[out_bs, out_bs, out_bs, sm_bs],
        out_shape=[jax.ShapeDtypeStruct((n, BRANCH_W), F32)] * 3 + [jax.ShapeDtypeStruct((n, LANES), F32)],
        scratch_shapes=[pltpu.VMEM((tt + 2 * HALO_D, BRANCH_W), F32),
                        pltpu.VMEM((min(DN_CONV_K, SUBLANES) - 1, tt + HALO_D, BRANCH_W), F32)],
        compiler_params=_cparams("arbitrary", "arbitrary"),
        name="mixer_d_prep",
    )(u, u, u, u, u, u, u, u, u, h, w_small, conv_w, alog_row, dt_row)


def _dir_prep(gb_ref, rev):
    c = DN_CHUNK
    ri = lax.broadcasted_iota(jnp.int32, (c, LANES), 0)
    ci = lax.broadcasted_iota(jnp.int32, (c, LANES), 1) % c
    incl = (ri <= ci) if rev else (ri >= ci)
    strict = (ri < ci) if rev else (ri > ci)
    gb = gb_ref[...]
    g1 = gb.astype(BF16)
    r1 = gb - g1.astype(F32)
    g2 = r1.astype(BF16)
    g3 = (r1 - g2.astype(F32)).astype(BF16)
    tri = incl[:, :c].astype(BF16)
    cs = _dot(jnp.concatenate([tri, tri, tri], axis=1), jnp.concatenate([g1, g2, g3], axis=0))
    last = 0 if rev else c - 1
    cs_last = cs[last:last + 1, :]
    return dict(incl=incl, strict=strict, eye=(ri == ci).astype(F32), gb=gb, cs=cs,
                cs_t=jnp.concatenate([cs, cs], axis=0).T, e_in=jnp.exp(cs), e_tail=jnp.exp(cs_last - cs),
                g_tot=jnp.exp(cs_last))


def _delta_step(dirs, s_ref):
    c = DN_CHUNK
    left = lax.broadcasted_iota(jnp.int32, (1, LANES), 1) < c
    chains = []
    for q_ref, k_ref, v_ref, gb_ref, o_ref, rev in dirs:
        dp = _dir_prep(gb_ref, rev)
        for h in range(DN_HEADS):
            lb = int(rev) * DN_HEADS + h
            chains.append(dict(dp=dp, sl=slice(h * DN_DK, (h + 1) * DN_DK), lb=lb, lg=2 * DN_HEADS + lb,
                               q_ref=q_ref, k_ref=k_ref, v_ref=v_ref, o_ref=o_ref))
    for ch in chains:
        dp, sl, lg = ch["dp"], ch["sl"], ch["lg"]
        q, k, v = ch["q_ref"][:, sl], ch["k_ref"][:, sl], ch["v_ref"][:, sl]
        beta = dp["gb"][:, ch["lb"]:ch["lb"] + 1]
        e_col = dp["e_in"][:, lg:lg + 1]
        kb = k * beta
        k16 = k.astype(BF16)
        kk = _dot_nt(jnp.concatenate([kb, q], axis=0).astype(BF16), jnp.concatenate([k16, k16], axis=0))
        decay = jnp.exp(jnp.where(dp["incl"], dp["cs"][:, lg:lg + 1] - dp["cs_t"][lg:lg + 1, :], NEG_INF))
        a_mat = jnp.where(dp["strict"], kk[:c] * decay, 0.0)
        ch.update(w=jnp.where(left, dp["eye"], -a_mat), attn=(kk[c:] * decay)[:, :c],
                  rhs=jnp.concatenate([v * beta, kb * e_col], axis=1).astype(BF16),
                  q_dec=q * e_col, k_tail_t=(k * dp["e_tail"][:, lg:lg + 1]).T)
    n = 1
    while n < c:
        for ch in chains:
            w = ch["w"]
            hi = w.astype(BF16)
            lo = (w - hi.astype(F32)).astype(BF16)
            q_mix = jnp.where(left, pltpu.roll(w, c, 1).astype(BF16), lo)
            r = _dot(jnp.concatenate([q_mix, q_mix], axis=1), jnp.concatenate([hi, hi, lo, lo], axis=0))
            ch["w"] = jnp.where(left, w + r, r)
        n *= 2
    for ch in chains:
        ch["uw"] = _dot(ch["w"][:, :c].astype(BF16), ch["rhs"])
    for i, ch in enumerate(chains):
        state = s_ref[i]
        sw = _dot(jnp.concatenate([ch["uw"][:, DN_DK:], ch["q_dec"]], axis=0).astype(BF16), state.astype(BF16))
        ch.update(state=state, v_new=ch["uw"][:, :DN_DK] - sw[:c], qs=sw[c:])
    for i, ch in enumerate(chains):
        lg = ch["lg"]
        fin = _dot(jnp.concatenate([ch["attn"], ch["k_tail_t"]], axis=0).astype(BF16), ch["v_new"].astype(BF16))
        ch["o_ref"][:, ch["sl"]] = ch["qs"] + fin[:c]
        s_ref[i] = ch["state"] * ch["dp"]["g_tot"][:, lg:lg + 1] + fin[c:]


def _delta_kernel(qf, kf, vf, gf, qb, kb, vb, gbb, of_ref, ob_ref, s_ref):
    @pl.when(pl.program_id(1) == 0)
    def _():
        s_ref[...] = jnp.zeros(s_ref.shape, F32)

    dirs = []
    for s in range(qf.shape[0]):
        dirs += [(qf.at[s], kf.at[s], vf.at[s], gf.at[s], of_ref.at[s], False),
                 (qb.at[s], kb.at[s], vb.at[s], gbb.at[s], ob_ref.at[s], True)]
    _delta_step(dirs, s_ref)


def _mixer_d_scan(q, k, v, gb, bsz, seq_len):
    n = q.shape[0]
    nc = seq_len // DN_CHUNK
    n_seq = DN_SEQ_PER_STEP if bsz % DN_SEQ_PER_STEP == 0 else 2
    assert bsz % n_seq == 0, (bsz, n_seq)
    grouped = lambda a: a.reshape(bsz // n_seq, n_seq, seq_len, a.shape[-1])
    fwd = lambda b, m: (b, 0, m, 0)
    bwd = lambda b, m: (b, 0, nc - 1 - m, 0)
    wide = lambda f: pl.BlockSpec((None, n_seq, DN_CHUNK, BRANCH_W), f)
    small = lambda f: pl.BlockSpec((None, n_seq, DN_CHUNK, LANES), f)
    q, k, v, gb = grouped(q), grouped(k), grouped(v), grouped(gb)
    o_f, o_b = pl.pallas_call(
        _delta_kernel,
        grid=(bsz // n_seq, nc),
        in_specs=[wide(fwd), wide(fwd), wide(fwd), small(fwd), wide(bwd), wide(bwd), wide(bwd), small(bwd)],
        out_specs=[wide(fwd), wide(bwd)],
        out_shape=[jax.ShapeDtypeStruct(q.shape, F32)] * 2,
        scratch_shapes=[pltpu.VMEM((2 * DN_HEADS * n_seq, DN_DK, DN_DK), F32)],
        compiler_params=_cparams("arbitrary", "arbitrary"),
        name="mixer_d_scan",
    )(q, k, v, gb, q, k, v, gb)
    return o_f.reshape(n, BRANCH_W), o_b.reshape(n, BRANCH_W)


def _dout_kernel(of_ref, ob_ref, gate_ref, ng_ref, o_ref):
    o = of_ref[...] + ob_ref[...]
    gate = _silu(gate_ref[...])
    for h in range(DN_HEADS):
        sl = slice(h * DN_DK, (h + 1) * DN_DK)
        oh = o[:, sl]
        inv = lax.rsqrt(jnp.mean(oh * oh, axis=-1, keepdims=True) + EPS)
        o_ref[:, sl] = (oh * inv * ng_ref[...] * gate[:, sl]).astype(o_ref.dtype)


def _mixer_d_out(o_f, o_b, u, norm_g):
    n = o_f.shape[0]
    tt = 512
    bs = pl.BlockSpec((tt, BRANCH_W), lambda i: (i, 0))
    return pl.pallas_call(
        _dout_kernel,
        grid=(n // tt,),
        in_specs=[bs, bs, pl.BlockSpec((tt, BRANCH_W), lambda i: (i, CB_D_GATE)),
                  pl.BlockSpec((1, DN_DK), lambda i: (0, 0))],
        out_specs=bs,
        out_shape=jax.ShapeDtypeStruct((n, BRANCH_W), BF16),
        compiler_params=_cparams("arbitrary"),
        name="mixer_d_out",
    )(o_f, o_b, u, norm_g)


def _merge_kernel(h_ref, ya, yb, yc, yd, wg0, wg1, wg2, wg3, wb_ref, bg_ref, o_ref):
    acc = None
    for b, (y_ref, wg_ref) in enumerate(zip((ya, yb, yc, yd), (wg0, wg1, wg2, wg3))):
        gate = _sigmoid(_dot(h_ref[...], wg_ref[...]) + bg_ref[b:b + 1, :])
        term = gate * _dot(y_ref[...], wb_ref[b])
        acc = term if acc is None else acc + term
    o_ref[...] = acc.astype(o_ref.dtype)


def _merge(h, ys, w_gate, w_branch, b_gate):
    n, d = h.shape
    tm, tn = ROW_TILE, 512
    nj = d // tn
    ybs = pl.BlockSpec((tm, BRANCH_W), lambda i, j: (i, 0))

    def wg_spec(b):
        return pl.BlockSpec((d, tn), lambda i, j: (0, b * nj + j))

    return pl.pallas_call(
        _merge_kernel,
        grid=(n // tm, nj),
        in_specs=[
            pl.BlockSpec((tm, d), lambda i, j: (i, 0)),
            ybs, ybs, ybs, ybs,
            wg_spec(0), wg_spec(1), wg_spec(2), wg_spec(3),
            pl.BlockSpec((N_BRANCH, BRANCH_W, tn), lambda i, j: (0, 0, j)),
            pl.BlockSpec((N_BRANCH, tn), lambda i, j: (0, j)),
        ],
        out_specs=pl.BlockSpec((tm, tn), lambda i, j: (i, j)),
        out_shape=jax.ShapeDtypeStruct((n, d), BF16),
        compiler_params=_cparams("arbitrary", "arbitrary"),
        name="branch_merge",
    )(h, *ys, w_gate, w_gate, w_gate, w_gate, w_branch, b_gate)


def _outproj_kernel(m_ref, w_ref, x_ref, mod_ref, o_ref):
    o_ref[...] = x_ref[...] + mod_ref[2:3, :] * _dot(m_ref[...], w_ref[...])


def _outproj(merged, w_out, x, mods, seq_len):
    n, d = x.shape
    tm, tn = ROW_TILE // 2, d
    tps = seq_len // tm
    return pl.pallas_call(
        _outproj_kernel,
        grid=(n // tm, d // tn),
        in_specs=[
            pl.BlockSpec((tm, d), lambda i, j: (i, 0)),
            pl.BlockSpec((d, tn), lambda i, j: (0, j)),
            pl.BlockSpec((tm, tn), lambda i, j: (i, j)),
            pl.BlockSpec((None, 6, tn), lambda i, j: (i // tps, 0, j)),
        ],
        out_specs=pl.BlockSpec((tm, tn), lambda i, j: (i, j)),
        out_shape=jax.ShapeDtypeStruct((n, d), F32),
        compiler_params=_cparams("arbitrary", "arbitrary"),
        name="out_proj",
    )(merged, w_out, x, mods)


def _ffn_kernel(x_ref, g_ref, mod_ref, w1_ref, w2_ref, fg_ref, o_ref, h_ref, *, final_norm):
    f = pl.program_id(1)

    @pl.when(f == 0)
    def _():
        _norm_mod(x_ref, g_ref, mod_ref, 3, 4, h_ref)
        o_ref[...] = x_ref[...]

    a = jnp.maximum(_dot(h_ref[...], w1_ref[...]), 0.0)
    o_ref[...] += mod_ref[5:6, :] * _dot((a * a).astype(BF16), w2_ref[...])

    if final_norm:
        @pl.when(f == pl.num_programs(1) - 1)
        def _():
            fg = fg_ref[...]

            group = NORM_ROWS * NORM_UNROLL

            def body(c, carry):
                r0 = pl.multiple_of(c * group, group)
                rows = [pl.ds(r0 + k * NORM_ROWS, NORM_ROWS) for k in range(NORM_UNROLL)]
                xs = [o_ref[r, :] for r in rows]
                ys = [xf * lax.rsqrt(jnp.mean(xf * xf, axis=-1, keepdims=True) + EPS) * fg for xf in xs]
                for r, y in zip(rows, ys):
                    o_ref[r, :] = y
                return carry

            lax.fori_loop(0, o_ref.shape[0] // group, body, 0)


def _ffn(x, g, mods, w1, w2, final_g, final_norm, seq_len):
    n, d = x.shape
    tm, tf = ROW_TILE, 512
    tps = seq_len // tm
    return pl.pallas_call(
        functools.partial(_ffn_kernel, final_norm=final_norm),
        grid=(n // tm, D_FF // tf),
        in_specs=[
            pl.BlockSpec((tm, d), lambda i, f: (i, 0)),
            pl.BlockSpec((1, d), lambda i, f: (0, 0)),
            pl.BlockSpec((None, 6, d), lambda i, f: (i // tps, 0, 0)),
            pl.BlockSpec((d, tf), lambda i, f: (0, f)),
            pl.BlockSpec((tf, d), lambda i, f: (f, 0)),
            pl.BlockSpec((1, d), lambda i, f: (0, 0)),
        ],
        out_specs=pl.BlockSpec((tm, d), lambda i, f: (i, 0)),
        out_shape=jax.ShapeDtypeStruct((n, d), F32),
        scratch_shapes=[pltpu.VMEM((tm, d), BF16)],
        compiler_params=_cparams("arbitrary", "arbitrary"),
        name="ffn",
    )(x, g, mods, w1, w2, final_g)


def _layer(x, mods, p, final_g, final_norm, bsz, seq_len):
    u, h = _inproj(x, p["norm_mix_g"], mods, p["w_in_main"], seq_len)
    y_a = _mixer_a(u, p["conv_a_w"], p["conv_a_b"], p["ln_a_g"], p["ln_a_b"], bsz, seq_len)
    y_b = _mixer_b(u, p["na_bias"], bsz, seq_len)
    y_c = _mixer_c(u, p["pool_w"], p["pool_scale"], bsz, seq_len)
    q, k, v, gb = _mixer_d_prep(u, h, p["w_in_small"], p["dn_conv_w"], p["dn_alog_row"], p["dn_dt_row"], bsz,
                                seq_len)
    o_f, o_b = _mixer_d_scan(q, k, v, gb, bsz, seq_len)
    y_d = _mixer_d_out(o_f, o_b, u, p["dn_norm_g"])
    merged = _merge(h, (y_a, y_b, y_c, y_d), p["w_gate"], p["w_branch"], p["b_gate"])
    x = _outproj(merged, p["w_out"], x, mods, seq_len)
    return _ffn(x, p["norm_ffn_g"], mods, p["w_ff1"], p["w_ff2"], final_g, final_norm, seq_len)


def _layer_params(l, norm_mix_g, norm_ffn_g, w_in, conv_a_w, conv_a_b, ln_a_g, ln_a_b, nat_rpb, pool_w, pool_scale,
                  dn_conv_w, dn_a_log, dn_dt_bias, dn_norm_g, w_branch, w_gate, b_gate, w_out, w_ff1, w_ff2):
    d = D_MODEL
    row = lambda a: a.reshape(1, -1).astype(F32)
    small_pad = ((0, 0), (0, LANES - U_SMALL))
    lane_pad = lambda a: jnp.pad(a.reshape(1, -1).astype(F32), ((0, 0), (2 * DN_HEADS, LANES - U_SMALL)))
    return {
        "norm_mix_g": row(norm_mix_g[l]), "norm_ffn_g": row(norm_ffn_g[l]),
        "w_in_main": w_in[l][:, :U_MAIN].astype(BF16),
        "w_in_small": jnp.pad(w_in[l][:, U_MAIN:], small_pad).astype(BF16),
        "conv_a_w": jnp.broadcast_to(conv_a_w[l].astype(F32)[:, None, :], (CONV_A_K, SUBLANES, BRANCH_W)),
        "conv_a_b": row(conv_a_b[l]),
        "ln_a_g": row(ln_a_g[l]), "ln_a_b": row(ln_a_b[l]),
        "na_bias": _na_bias_table(nat_rpb[l]),
        "pool_w": pool_w[l].astype(BF16), "pool_scale": row(pool_scale[l]),
        "dn_conv_w": dn_conv_w[l].astype(F32),
        "dn_alog_row": lane_pad(dn_a_log[l]), "dn_dt_row": lane_pad(dn_dt_bias[l]),
        "dn_norm_g": row(dn_norm_g[l]),
        "w_branch": w_branch[l].astype(BF16),
        "w_gate": w_gate[l].reshape(d, N_BRANCH * d).astype(BF16),
        "b_gate": b_gate[l].astype(F32),
        "w_out": w_out[l].astype(BF16), "w_ff1": w_ff1[l].astype(BF16), "w_ff2": w_ff2[l].astype(BF16),
    }


def kernel(x_prompt, x_sample, c_prompt, c_sample, norm_mix_g, norm_ffn_g, w_ada, b_ada, w_in, conv_a_w, conv_a_b,
           ln_a_g, ln_a_b, nat_rpb, pool_w, pool_scale, dn_conv_w, dn_a_log, dn_dt_bias, dn_norm_g, w_branch, w_gate,
           b_gate, w_out, w_ff1, w_ff2, final_norm_g):
    depth = w_in.shape[0]
    layers = [_layer_params(l, norm_mix_g, norm_ffn_g, w_in, conv_a_w, conv_a_b, ln_a_g, ln_a_b, nat_rpb, pool_w,
                            pool_scale, dn_conv_w, dn_a_log, dn_dt_bias, dn_norm_g, w_branch, w_gate, b_gate, w_out,
                            w_ff1, w_ff2) for l in range(depth)]
    n_prompt = c_prompt.shape[0]
    mods = _ada(jnp.concatenate([c_prompt, c_sample], axis=0), w_ada, b_ada)
    fg = final_norm_g.reshape(1, -1).astype(F32)
    outs = []
    for x, lo, hi in ((x_prompt, 0, n_prompt), (x_sample, n_prompt, mods.shape[1])):
        bsz, seq_len, d = x.shape
        xt = x.reshape(bsz * seq_len, d)
        for l in range(depth):
            xt = _layer(xt, mods[l, lo:hi], layers[l], fg, l == depth - 1, bsz, seq_len)
        outs.append(xt.reshape(bsz, seq_len, d))
    return tuple(outs)
```
